```python
import jax, jax.numpy as jnp
from jax import lax
import numpy as np

D_MODEL = 1024
BATCH = 4
SEQ = 8192
DEPTH = 4

N_GROUPS = 4
FOURIER_WIDTH = 512
FOURIER_GROUP = FOURIER_WIDTH // N_GROUPS
CONV_WIDTH = 512
CONV_K = 3
POOL_WIDTH = 512
POOL_GROUP = POOL_WIDTH // N_GROUPS
POOL_WINDOWS = (2, 4, 8, 16)
OUT_GROUP = D_MODEL // N_GROUPS
N_BRANCHES = 3
D_FF = 2816
EPS = 1e-6

OFF_F = 0
OFF_B = OFF_F + FOURIER_WIDTH
OFF_C = OFF_B + CONV_WIDTH
OFF_V = OFF_C + CONV_WIDTH
OFF_P = OFF_V + CONV_WIDTH
OFF_G = OFF_P + POOL_WIDTH
IN_WIDTH = OFF_G + N_BRANCHES * D_MODEL

kernel_name = "hybrid_fourier_conv_pool_macaron_encoder"


def rmsnorm(x, g):
    xf = x.astype(jnp.float32)
    y = xf * lax.rsqrt(jnp.mean(xf * xf, axis=-1, keepdims=True) + EPS)
    return (y * g.astype(jnp.float32)).astype(x.dtype)


def swiglu(h, w1, w3, w2):
    return (jax.nn.silu(h @ w1) * (h @ w3)) @ w2


def fourier_mix(u, w_map):
    b, s, _ = u.shape
    ug = u.reshape(b, s, N_GROUPS, FOURIER_GROUP).astype(jnp.float32)
    f = jnp.fft.fftn(ug, axes=(1, 3), norm="ortho").real.astype(u.dtype)
    y = jnp.einsum("bsgc,gcd->bsgd", f, w_map)
    return y.reshape(b, s, D_MODEL)


def short_conv(bg, cg, v, w_conv, w_out):
    z = cg * v
    zp = jnp.pad(z, ((0, 0), (1, 1), (0, 0)))
    conv = w_conv[0] * zp[:, :-2] + w_conv[1] * zp[:, 1:-1] + w_conv[2] * zp[:, 2:]
    return (bg * conv) @ w_out


def pool_mix(u, w_map, scale):
    s = u.shape[1]
    t = jnp.arange(s, dtype=jnp.float32)
    outs = []
    for i, w in enumerate(POOL_WINDOWS):
        half = w // 2
        ug = u[..., i * POOL_GROUP:(i + 1) * POOL_GROUP].astype(jnp.float32)
        cs = jnp.pad(jnp.cumsum(ug, axis=1), ((0, 0), (1, 0), (0, 0)))
        padded = jnp.pad(cs, ((0, 0), (half, half), (0, 0)), mode="edge")
        win = padded[:, w:w + s] - padded[:, :s]
        count = jnp.minimum(t + half, float(s)) - jnp.maximum(t - half, 0.0)
        pooled = win / count[None, :, None] - ug
        outs.append(pooled.astype(u.dtype) @ w_map[i])
    return jnp.concatenate(outs, axis=-1) * scale


def setup_inputs(seed: int = 0) -> dict:
    key = jax.random.key(seed)
    ks = jax.random.split(key, 20)
    f32 = jnp.float32

    def nrm(k, shape, fan_in):
        return jax.random.normal(k, shape, f32) * (fan_in ** -0.5)

    def gain(k, shape):
        return 1.0 + 0.02 * jax.random.normal(k, shape, f32)

    L, D = DEPTH, D_MODEL
    return {
        "x": jax.random.normal(ks[0], (BATCH, SEQ, D), f32),
        "g_ffn1": gain(ks[1], (L, D)),
        "w1_a": nrm(ks[2], (L, D, D_FF), D),
        "w3_a": nrm(ks[3], (L, D, D_FF), D),
        "w2_a": nrm(ks[4], (L, D_FF, D), D_FF),
        "g_mix": gain(ks[5], (L, D)),
        "w_in": nrm(ks[6], (L, D, IN_WIDTH), D),
        "w_fourier": nrm(ks[7], (L, N_GROUPS, FOURIER_GROUP, OUT_GROUP), FOURIER_GROUP),
        "w_conv": nrm(ks[8], (L, CONV_K, CONV_WIDTH), CONV_K),
        "w_conv_out": nrm(ks[9], (L, CONV_WIDTH, D), CONV_WIDTH),
        "w_pool": nrm(ks[10], (L, N_GROUPS, POOL_GROUP, OUT_GROUP), POOL_GROUP),
        "pool_scale": gain(ks[11], (L, D)),
        "w_o": nrm(ks[12], (L, D, D), D),
        "g_ffn2": gain(ks[13], (L, D)),
        "w1_b": nrm(ks[14], (L, D, D_FF), D),
        "w3_b": nrm(ks[15], (L, D, D_FF), D),
        "w2_b": nrm(ks[16], (L, D_FF, D), D_FF),
        "g_final": gain(ks[17], (D,)),
    }


def reference(x, g_ffn1, w1_a, w3_a, w2_a, g_mix, w_in, w_fourier, w_conv, w_conv_out,
              w_pool, pool_scale, w_o, g_ffn2, w1_b, w3_b, w2_b, g_final):
    b, s, d = x.shape
    for l in range(DEPTH):
        h = rmsnorm(x, g_ffn1[l])
        x = x + 0.5 * swiglu(h, w1_a[l], w3_a[l], w2_a[l])

        h = rmsnorm(x, g_mix[l])
        p = h @ w_in[l]
        gates = jax.nn.sigmoid(p[..., OFF_G:].astype(jnp.float32)).astype(x.dtype)
        gates = gates.reshape(b, s, N_BRANCHES, d)
        y_f = fourier_mix(p[..., OFF_F:OFF_B], w_fourier[l])
        y_c = short_conv(p[..., OFF_B:OFF_C], p[..., OFF_C:OFF_V], p[..., OFF_V:OFF_P],
                         w_conv[l], w_conv_out[l])
        y_p = pool_mix(p[..., OFF_P:OFF_G], w_pool[l], pool_scale[l])
        merged = gates[:, :, 0] * y_f + gates[:, :, 1] * y_c + gates[:, :, 2] * y_p
        x = x + merged @ w_o[l]

        h = rmsnorm(x, g_ffn2[l])
        x = x + 0.5 * swiglu(h, w1_b[l], w3_b[l], w2_b[l])
    return rmsnorm(x, g_final)
```

```python
import functools
import math

import jax
import jax.numpy as jnp
from jax import lax
from jax.experimental import pallas as pl
from jax.experimental.pallas import tpu as pltpu

D_MODEL = 1024
SEQ = 8192
DEPTH = 4
N_GROUPS = 4
FOURIER_WIDTH = 512
FOURIER_GROUP = FOURIER_WIDTH // N_GROUPS
CONV_WIDTH = 512
POOL_WIDTH = 512
POOL_GROUP = POOL_WIDTH // N_GROUPS
POOL_WINDOWS = (2, 4, 8, 16)
OUT_GROUP = D_MODEL // N_GROUPS
D_FF = 2816
EPS = 1e-6

OFF_F = 0
OFF_B = OFF_F + FOURIER_WIDTH
OFF_C = OFF_B + CONV_WIDTH
OFF_V = OFF_C + CONV_WIDTH
OFF_P = OFF_V + CONV_WIDTH
OFF_G = OFF_P + POOL_WIDTH
IN_WIDTH = OFF_G + 3 * D_MODEL

SUBLANES = 8
MXU_COLS = 256
VMEM_LIMIT = 56 * 1024 * 1024

DFT_N1 = 64
DFT_N2 = SEQ // DFT_N1
OUT_TOK = DFT_N2 * SUBLANES

HALO = 8
FFN_TM = 512
MIX_TM = 512

BF16 = jnp.bfloat16
F32 = jnp.float32


def _dot(a, b):
    return jnp.dot(a, b, preferred_element_type=F32)


def _rmsnorm_bf16(x, g):
    ms = jnp.mean(x * x, axis=-1, keepdims=True)
    return (x * lax.rsqrt(ms + EPS) * g).astype(BF16)


def _resident(shape):
    nd = len(shape)
    return pl.BlockSpec(shape, lambda *_: (0,) * nd, pipeline_mode=pl.Buffered(1))


def _ffn_kernel(x_ref, g_ref, w1_ref, w3_ref, w2_ref, gfin_ref, o_ref, act_ref, *, final_norm):
    x = x_ref[...]
    h = _rmsnorm_bf16(x, g_ref[...])
    for c in range(0, D_FF, MXU_COLS):
        a = _dot(h, w1_ref[:, c:c + MXU_COLS])
        b = _dot(h, w3_ref[:, c:c + MXU_COLS])
        act_ref[:, c:c + MXU_COLS] = (jax.nn.silu(a) * b).astype(BF16)
    y = x + 0.5 * _dot(act_ref[...], w2_ref[...])
    if final_norm:
        ms = jnp.mean(y * y, axis=-1, keepdims=True)
        y = y * lax.rsqrt(ms + EPS) * gfin_ref[...]
    o_ref[...] = y


def _ffn(x2d, g, w1, w3, w2, gfin, final_norm):
    t, d = x2d.shape
    tm = FFN_TM
    return pl.pallas_call(
        functools.partial(_ffn_kernel, final_norm=final_norm),
        out_shape=jax.ShapeDtypeStruct((t, d), F32),
        grid=(t // tm,),
        in_specs=[
            pl.BlockSpec((tm, d), lambda i: (i, 0)),
            _resident((1, d)),
            _resident((d, D_FF)),
            _resident((d, D_FF)),
            _resident((D_FF, d)),
            _resident((1, d)),
        ],
        out_specs=pl.BlockSpec((tm, d), lambda i: (i, 0)),
        scratch_shapes=[pltpu.VMEM((tm, D_FF), BF16)],
        compiler_params=pltpu.CompilerParams(
            dimension_semantics=("arbitrary",), vmem_limit_bytes=VMEM_LIMIT),
        name="ffn_final" if final_norm else "ffn",
    )(x2d, g, w1, w3, w2, gfin)


def _mix_in_kernel(x_ref, xp_ref, xn_ref, g_ref, win_ref, wcd_ref, wconv_ref, wco_ref,
                   wpool_ref, pscale_ref, z_ref, rest_ref, ext_ref):
    tm = MIX_TM
    i = pl.program_id(1)
    nt = pl.num_programs(1)
    g = g_ref[...]
    h = _rmsnorm_bf16(x_ref[0], g)
    hh = _rmsnorm_bf16(jnp.concatenate([xp_ref[0], xn_ref[0]], axis=0), g)

    pf = _dot(h, win_ref[:, OFF_F:OFF_B]).astype(BF16)
    for q in range(N_GROUPS):
        lo = q * FOURIER_GROUP
        zz = _dot(pf[:, lo:lo + FOURIER_GROUP], wcd_ref[...])
        z_ref[0, 0, :, lo:lo + FOURIER_GROUP] = zz[:, :FOURIER_GROUP]
        z_ref[0, 1, :, lo:lo + FOURIER_GROUP] = zz[:, FOURIER_GROUP:]

    ph = _dot(hh, win_ref[:, OFF_C:OFF_G])
    halo = jnp.concatenate(
        [ph[:, 0:CONV_WIDTH] * ph[:, CONV_WIDTH:2 * CONV_WIDTH], ph[:, 2 * CONV_WIDTH:]], axis=1)
    ext_ref[0:HALO, :] = jnp.where(i > 0, halo[0:HALO], 0.0)
    ext_ref[HALO + tm:, :] = jnp.where(i < nt - 1, halo[HALO:], 0.0)
    zc = _dot(h, win_ref[:, OFF_C:OFF_V]) * _dot(h, win_ref[:, OFF_V:OFF_P])
    ext_ref[HALO:HALO + tm, 0:CONV_WIDTH] = zc
    ext_ref[HALO:HALO + tm, CONV_WIDTH:] = _dot(h, win_ref[:, OFF_P:OFF_G])

    wconv = wconv_ref[...]
    conv = (wconv[0:1] * ext_ref[HALO - 1:HALO - 1 + tm, 0:CONV_WIDTH]
            + wconv[1:2] * zc
            + wconv[2:3] * ext_ref[HALO + 1:HALO + 1 + tm, 0:CONV_WIDTH])
    bconv = (_dot(h, win_ref[:, OFF_B:OFF_C]) * conv).astype(BF16)

    t = (i * tm + lax.broadcasted_iota(jnp.int32, (tm, 1), 0)).astype(F32)
    pscale = pscale_ref[...]
    for q in range(N_GROUPS):
        half = POOL_WINDOWS[q] // 2
        lo = CONV_WIDTH + q * POOL_GROUP
        win = ext_ref[HALO - half:HALO - half + tm, lo:lo + POOL_GROUP]
        for j in range(1 - half, half):
            win = win + ext_ref[HALO + j:HALO + j + tm, lo:lo + POOL_GROUP]
        count = jnp.minimum(t + half, float(SEQ)) - jnp.maximum(t - half, 0.0)
        pooled = (win / count - ext_ref[HALO:HALO + tm, lo:lo + POOL_GROUP]).astype(BF16)

        cols = slice(q * OUT_GROUP, (q + 1) * OUT_GROUP)
        gcol = OFF_G + D_MODEL + q * OUT_GROUP
        gate_c = jax.nn.sigmoid(_dot(h, win_ref[:, gcol:gcol + OUT_GROUP]))
        gate_p = jax.nn.sigmoid(_dot(h, win_ref[:, gcol + D_MODEL:gcol + D_MODEL + OUT_GROUP]))
        y_c = _dot(bconv, wco_ref[:, cols])
        y_p = _dot(pooled, wpool_ref[q]) * pscale[:, cols]
        rest_ref[0, :, cols] = gate_c * y_c + gate_p * y_p


def _mix_in(x, g, w_in, wcd, w_conv, w_co, w_pool, pscale):
    b, s, d = x.shape
    tm = MIX_TM
    nt = s // tm
    hb = tm // HALO
    last_hb = s // HALO - 1
    return pl.pallas_call(
        _mix_in_kernel,
        out_shape=(
            jax.ShapeDtypeStruct((b, 2, s, FOURIER_WIDTH), F32),
            jax.ShapeDtypeStruct((b, s, d), F32),
        ),
        grid=(b, nt),
        in_specs=[
            pl.BlockSpec((1, tm, d), lambda bi, i: (bi, i, 0)),
            pl.BlockSpec((1, HALO, d), lambda bi, i: (bi, jnp.maximum(i * hb - 1, 0), 0)),
            pl.BlockSpec((1, HALO, d), lambda bi, i: (bi, jnp.minimum((i + 1) * hb, last_hb), 0)),
            _resident((1, d)),
            _resident((d, IN_WIDTH)),
            _resident((FOURIER_GROUP, 2 * FOURIER_GROUP)),
            _resident((3, CONV_WIDTH)),
            _resident((CONV_WIDTH, d)),
            _resident((N_GROUPS, POOL_GROUP, OUT_GROUP)),
            _resident((1, d)),
        ],
        out_specs=(
            pl.BlockSpec((1, 2, tm, FOURIER_WIDTH), lambda bi, i: (bi, 0, i, 0)),
            pl.BlockSpec((1, tm, d), lambda bi, i: (bi, i, 0)),
        ),
        scratch_shapes=[pltpu.VMEM((tm + 2 * HALO, CONV_WIDTH + POOL_WIDTH), F32)],
        compiler_params=pltpu.CompilerParams(
            dimension_semantics=("arbitrary", "arbitrary"), vmem_limit_bytes=VMEM_LIMIT),
        name="mix_in",
    )(x, x, x, g, w_in, wcd, w_conv, w_co, w_pool, pscale)


def _fft1_kernel(w_ref, z_ref, o_ref):
    rows = 2 * DFT_N1 * SUBLANES
    zz = z_ref[0].reshape(rows, FOURIER_WIDTH).astype(BF16)
    o_ref[0] = _dot(w_ref[...], zz).reshape(2, DFT_N1, SUBLANES, FOURIER_WIDTH)


def _fft1(z5, w1k):
    b = z5.shape[0]
    blk = (1, 2, DFT_N1, SUBLANES, FOURIER_WIDTH)
    idx = lambda bi, j: (bi, 0, 0, j, 0)
    return pl.pallas_call(
        _fft1_kernel,
        out_shape=jax.ShapeDtypeStruct(z5.shape, F32),
        grid=(b, DFT_N2 // SUBLANES),
        in_specs=[_resident(w1k.shape), pl.BlockSpec(blk, idx)],
        out_specs=pl.BlockSpec(blk, idx),
        compiler_params=pltpu.CompilerParams(
            dimension_semantics=("arbitrary", "arbitrary"), vmem_limit_bytes=VMEM_LIMIT),
        name="fft1",
    )(w1k, z5)


def _mix_out_kernel(x_ref, rest_ref, y_ref, gb_ref, g_ref, wgf_ref, wf_ref, wo_ref, o_ref, m_ref):
    t = OUT_TOK
    x = x_ref[0].reshape(t, D_MODEL)
    yy = y_ref[0].reshape(2 * t, FOURIER_WIDTH).astype(BF16)
    f = _dot(gb_ref[0], yy).astype(BF16)
    h = _rmsnorm_bf16(x, g_ref[...])
    rest = rest_ref[0].reshape(t, D_MODEL)
    for q in range(N_GROUPS):
        cols = slice(q * OUT_GROUP, (q + 1) * OUT_GROUP)
        gate_f = jax.nn.sigmoid(_dot(h, wgf_ref[:, cols]))
        y_f = _dot(f[:, q * FOURIER_GROUP:(q + 1) * FOURIER_GROUP], wf_ref[q])
        m_ref[:, cols] = (gate_f * y_f + rest[:, cols]).astype(BF16)
    o = x + _dot(m_ref[...], wo_ref[...])
    o_ref[0] = o.reshape(DFT_N2, SUBLANES, D_MODEL)


def _mix_out(x4, rest4, y5, gbig, g, wgf, wf, wo):
    b = x4.shape[0]
    d = D_MODEL
    tok_blk = (1, DFT_N2, SUBLANES, d)
    tok = lambda ki, bi: (bi, 0, ki, 0)
    return pl.pallas_call(
        _mix_out_kernel,
        out_shape=jax.ShapeDtypeStruct(x4.shape, F32),
        grid=(DFT_N1 // SUBLANES, b),
        in_specs=[
            pl.BlockSpec(tok_blk, tok),
            pl.BlockSpec(tok_blk, tok),
            pl.BlockSpec((1, 2, SUBLANES, DFT_N2, FOURIER_WIDTH), lambda ki, bi: (bi, 0, ki, 0, 0)),
            pl.BlockSpec((1, OUT_TOK, 2 * OUT_TOK), lambda ki, bi: (ki, 0, 0),
                         pipeline_mode=pl.Buffered(1)),
            _resident((1, d)),
            _resident((d, d)),
            _resident((N_GROUPS, FOURIER_GROUP, OUT_GROUP)),
            _resident((d, d)),
        ],
        out_specs=pl.BlockSpec(tok_blk, tok),
        scratch_shapes=[pltpu.VMEM((OUT_TOK, d), BF16)],
        compiler_params=pltpu.CompilerParams(
            dimension_semantics=("arbitrary", "arbitrary"), vmem_limit_bytes=VMEM_LIMIT),
        name="mix_out",
    )(x4, rest4, y5, gbig, g, wgf, wf, wo)


def _dft_tables():
    two_pi = 2.0 * math.pi
    r8 = SUBLANES
    c = lax.broadcasted_iota(jnp.int32, (FOURIER_GROUP, FOURIER_GROUP), 0)
    k = lax.broadcasted_iota(jnp.int32, (FOURIER_GROUP, FOURIER_GROUP), 1)
    ang = ((c * k) % FOURIER_GROUP).astype(F32) * (two_pi / FOURIER_GROUP)
    wcd = jnp.concatenate([jnp.cos(ang), -jnp.sin(ang)], axis=1).astype(BF16)

    k1 = lax.broadcasted_iota(jnp.int32, (DFT_N1, DFT_N1), 0)
    n1 = lax.broadcasted_iota(jnp.int32, (DFT_N1, DFT_N1), 1)
    ang = ((k1 * n1) % DFT_N1).astype(F32) * (two_pi / DFT_N1)
    cs, sn = jnp.cos(ang), jnp.sin(ang)
    w1 = jnp.stack([jnp.stack([cs, sn], axis=1), jnp.stack([-sn, cs], axis=1)], axis=0)
    eye = jnp.eye(r8, dtype=F32)
    w1k = (w1[:, :, None, :, :, None] * eye[None, None, :, None, None, :]).reshape(
        2 * DFT_N1 * r8, 2 * DFT_N1 * r8).astype(BF16)

    nkb = DFT_N1 // r8
    shp = (nkb, DFT_N2, r8, DFT_N2)
    kk = (r8 * lax.broadcasted_iota(jnp.int32, shp, 0) + lax.broadcasted_iota(jnp.int32, shp, 2)
          + DFT_N1 * lax.broadcasted_iota(jnp.int32, shp, 1))
    nn = lax.broadcasted_iota(jnp.int32, shp, 3)
    ang = ((kk * nn) % SEQ).astype(F32) * (two_pi / SEQ)
    norm = 1.0 / math.sqrt(FOURIER_GROUP * SEQ)
    gd = jnp.stack([jnp.cos(ang), jnp.sin(ang)], axis=3) * norm
    gbig = (gd[:, :, :, :, None, :] * eye[None, None, :, None, :, None]).reshape(
        nkb, OUT_TOK, 2 * OUT_TOK).astype(BF16)
    return wcd, w1k, gbig


def kernel(x, g_ffn1, w1_a, w3_a, w2_a, g_mix, w_in, w_fourier, w_conv, w_conv_out,
           w_pool, pool_scale, w_o, g_ffn2, w1_b, w3_b, w2_b, g_final):
    b, s, d = x.shape
    assert (s, d) == (SEQ, D_MODEL)
    wcd, w1k, gbig = _dft_tables()
    gfin = g_final.reshape(1, d)
    for l in range(DEPTH):
        x = _ffn(x.reshape(b * s, d), g_ffn1[l].reshape(1, d), w1_a[l].astype(BF16),
                 w3_a[l].astype(BF16), w2_a[l].astype(BF16), gfin, False).reshape(b, s, d)
        gm = g_mix[l].reshape(1, d)
        w_in_l = w_in[l].astype(BF16)
        z, rest = _mix_in(x, gm, w_in_l, wcd, w_conv[l], w_conv_out[l].astype(BF16),
                          w_pool[l].astype(BF16), pool_scale[l].reshape(1, d))
        y = _fft1(z.reshape(b, 2, DFT_N1, DFT_N2, FOURIER_WIDTH), w1k)
        view = (b, DFT_N2, DFT_N1, d)
        x = _mix_out(x.reshape(view), rest.reshape(view), y, gbig, gm,
                     w_in_l[:, OFF_G:OFF_G + d], w_fourier[l].astype(BF16),
                     w_o[l].astype(BF16)).reshape(b, s, d)
        x = _ffn(x.reshape(b * s, d), g_ffn2[l].reshape(1, d), w1_b[l].astype(BF16),
                 w3_b[l].astype(BF16), w2_b[l].astype(BF16), gfin,
                 l == DEPTH - 1).reshape(b, s, d)
    return x
```

```python
import functools
import math

import numpy as np
import jax
import jax.numpy as jnp
from jax import lax
from jax.experimental import pallas as pl
from jax.experimental.pallas import tpu as pltpu

D_MODEL = 1024
SEQ = 8192
DEPTH = 4
N_GROUPS = 4
FOURIER_WIDTH = 512
FOURIER_GROUP = FOURIER_WIDTH // N_GROUPS
CONV_WIDTH = 512
POOL_WIDTH = 512
POOL_GROUP = POOL_WIDTH // N_GROUPS
POOL_WINDOWS = (2, 4, 8, 16)
OUT_GROUP = D_MODEL // N_GROUPS
D_FF = 2816
EPS = 1e-6

OFF_F = 0
OFF_B = OFF_F + FOURIER_WIDTH
OFF_C = OFF_B + CONV_WIDTH
OFF_V = OFF_C + CONV_WIDTH
OFF_P = OFF_V + CONV_WIDTH
OFF_G = OFF_P + POOL_WIDTH
IN_WIDTH = OFF_G + 3 * D_MODEL

SUBLANES = 8
LANES = 128
MXU_COLS = 256
VMEM_LIMIT = 56 * 1024 * 1024

DFT_N1 = 64
DFT_N2 = SEQ // DFT_N1
FFT1_ROWS = DFT_N1 * SUBLANES
OUT_TOK = DFT_N2 * SUBLANES
GATE_BLK = 512

HALO = 8
FFN_TM = 512
MIX_TM = 512

BF16 = jnp.bfloat16
F32 = jnp.float32


def _dot(a, b):
    return jnp.dot(a, b, preferred_element_type=F32)


def _rmsnorm_bf16(x, g):
    ms = jnp.mean(x * x, axis=-1, keepdims=True)
    return (x * lax.rsqrt(ms + EPS) * g).astype(BF16)


def _resident(shape):
    nd = len(shape)
    return pl.BlockSpec(shape, lambda *_: (0,) * nd, pipeline_mode=pl.Buffered(1))


def _layer(l, shape, col=0):
    return pl.BlockSpec((None,) + shape, lambda *_: (l, 0, col), pipeline_mode=pl.Buffered(1))


def _ffn_kernel(x_ref, g_ref, w1_ref, w3_ref, w2_ref, gfin_ref, o_ref, act_ref, *, final_norm):
    x = x_ref[...]
    h = _rmsnorm_bf16(x, g_ref[...])
    for c in range(0, D_FF, MXU_COLS):
        a = _dot(h, w1_ref[:, c:c + MXU_COLS])
        b = _dot(h, w3_ref[:, c:c + MXU_COLS])
        act_ref[:, c:c + MXU_COLS] = (jax.nn.silu(a) * b).astype(BF16)
    y = x + 0.5 * _dot(act_ref[...], w2_ref[...])
    if final_norm:
        ms = jnp.mean(y * y, axis=-1, keepdims=True)
        y = y * lax.rsqrt(ms + EPS) * gfin_ref[...]
    o_ref[...] = y


def _ffn(x2d, l, g, w1, w3, w2, gfin, final_norm):
    t, d = x2d.shape
    tm = FFN_TM
    return pl.pallas_call(
        functools.partial(_ffn_kernel, final_norm=final_norm),
        out_shape=jax.ShapeDtypeStruct((t, d), F32),
        grid=(t // tm,),
        in_specs=[
            pl.BlockSpec((tm, d), lambda i: (i, 0)),
            _layer(l, (1, d)),
            _layer(l, (d, D_FF)),
            _layer(l, (d, D_FF)),
            _layer(l, (D_FF, d)),
            _resident((1, d)),
        ],
        out_specs=pl.BlockSpec((tm, d), lambda i: (i, 0)),
        scratch_shapes=[pltpu.VMEM((tm, D_FF), BF16)],
        compiler_params=pltpu.CompilerParams(
            dimension_semantics=("arbitrary",), vmem_limit_bytes=VMEM_LIMIT),
        name="ffn_final" if final_norm else "ffn",
    )(x2d, g, w1, w3, w2, gfin)


def _mix_in_kernel(x_ref, xp_ref, xn_ref, g_ref, win_ref, wconv_ref, wco_ref,
                   wpool_ref, pscale_ref, pf_ref, rest_ref, ext_ref):
    tm = MIX_TM
    i = pl.program_id(1)
    nt = pl.num_programs(1)
    g = g_ref[...]
    h = _rmsnorm_bf16(x_ref[0], g)
    hh = _rmsnorm_bf16(jnp.concatenate([xp_ref[0], xn_ref[0]], axis=0), g)

    pf_ref[0] = _dot(h, win_ref[:, OFF_F:OFF_B])

    ph = _dot(hh, win_ref[:, OFF_C:OFF_G])
    halo = jnp.concatenate(
        [ph[:, 0:CONV_WIDTH] * ph[:, CONV_WIDTH:2 * CONV_WIDTH], ph[:, 2 * CONV_WIDTH:]], axis=1)
    ext_ref[0:HALO, :] = jnp.where(i > 0, halo[0:HALO], 0.0)
    ext_ref[HALO + tm:, :] = jnp.where(i < nt - 1, halo[HALO:], 0.0)
    zc = _dot(h, win_ref[:, OFF_C:OFF_V]) * _dot(h, win_ref[:, OFF_V:OFF_P])
    ext_ref[HALO:HALO + tm, 0:CONV_WIDTH] = zc
    ext_ref[HALO:HALO + tm, CONV_WIDTH:] = _dot(h, win_ref[:, OFF_P:OFF_G])

    wconv = wconv_ref[...]
    conv = (wconv[0:1] * ext_ref[HALO - 1:HALO - 1 + tm, 0:CONV_WIDTH]
            + wconv[1:2] * zc
            + wconv[2:3] * ext_ref[HALO + 1:HALO + 1 + tm, 0:CONV_WIDTH])
    bconv = (_dot(h, win_ref[:, OFF_B:OFF_C]) * conv).astype(BF16)

    t = (i * tm + lax.broadcasted_iota(jnp.int32, (tm, 1), 0)).astype(F32)
    pscale = pscale_ref[...]
    for q in range(N_GROUPS):
        half = POOL_WINDOWS[q] // 2
        lo = CONV_WIDTH + q * POOL_GROUP
        win = ext_ref[HALO - half:HALO - half + tm, lo:lo + POOL_GROUP]
        for j in range(1 - half, half):
            win = win + ext_ref[HALO + j:HALO + j + tm, lo:lo + POOL_GROUP]
        count = jnp.minimum(t + half, float(SEQ)) - jnp.maximum(t - half, 0.0)
        pooled = (win / count - ext_ref[HALO:HALO + tm, lo:lo + POOL_GROUP]).astype(BF16)

        cols = slice(q * OUT_GROUP, (q + 1) * OUT_GROUP)
        gcol = OFF_G + D_MODEL + q * OUT_GROUP
        gate_c = jax.nn.sigmoid(_dot(h, win_ref[:, gcol:gcol + OUT_GROUP]))
        gate_p = jax.nn.sigmoid(_dot(h, win_ref[:, gcol + D_MODEL:gcol + D_MODEL + OUT_GROUP]))
        y_c = _dot(bconv, wco_ref[:, cols])
        y_p = _dot(pooled, wpool_ref[q]) * pscale[:, cols]
        rest_ref[0, :, cols] = gate_c * y_c + gate_p * y_p


def _mix_in(x, l, g, w_in, w_conv, w_co, w_pool, pscale):
    b, s, d = x.shape
    tm = MIX_TM
    nt = s // tm
    hb = tm // HALO
    last_hb = s // HALO - 1
    return pl.pallas_call(
        _mix_in_kernel,
        out_shape=(
            jax.ShapeDtypeStruct((b, s, FOURIER_WIDTH), F32),
            jax.ShapeDtypeStruct((b, s, d), F32),
        ),
        grid=(b, nt),
        in_specs=[
            pl.BlockSpec((1, tm, d), lambda bi, i: (bi, i, 0)),
            pl.BlockSpec((1, HALO, d), lambda bi, i: (bi, jnp.maximum(i * hb - 1, 0), 0)),
            pl.BlockSpec((1, HALO, d), lambda bi, i: (bi, jnp.minimum((i + 1) * hb, last_hb), 0)),
            _layer(l, (1, d)),
            _layer(l, (d, IN_WIDTH)),
            _layer(l, (3, CONV_WIDTH)),
            _layer(l, (CONV_WIDTH, d)),
            pl.BlockSpec((None, N_GROUPS, POOL_GROUP, OUT_GROUP), lambda *_: (l, 0, 0, 0),
                         pipeline_mode=pl.Buffered(1)),
            _layer(l, (1, d)),
        ],
        out_specs=(
            pl.BlockSpec((1, tm, FOURIER_WIDTH), lambda bi, i: (bi, i, 0)),
            pl.BlockSpec((1, tm, d), lambda bi, i: (bi, i, 0)),
        ),
        scratch_shapes=[pltpu.VMEM((tm + 2 * HALO, CONV_WIDTH + POOL_WIDTH), F32)],
        compiler_params=pltpu.CompilerParams(
            dimension_semantics=("arbitrary", "arbitrary"), vmem_limit_bytes=VMEM_LIMIT),
        name="mix_in",
    )(x, x, x, g, w_in, w_conv, w_co, w_pool, pscale)


def _fft1_kernel(w_ref, wcc_ref, p_ref, o_ref):
    rows = FFT1_ROWS
    pf = p_ref[0].reshape(rows, FOURIER_WIDTH).astype(BF16)
    a = _dot(w_ref[...], pf).astype(BF16)
    for q in range(N_GROUPS):
        cols = slice(q * FOURIER_GROUP, (q + 1) * FOURIER_GROUP)
        y = _dot(jnp.concatenate([a[:rows, cols], a[rows:, cols]], axis=1), wcc_ref[...])
        o_ref[0, 0, :, :, cols] = y[:, :FOURIER_GROUP].reshape(DFT_N1, SUBLANES, FOURIER_GROUP)
        o_ref[0, 1, :, :, cols] = y[:, FOURIER_GROUP:].reshape(DFT_N1, SUBLANES, FOURIER_GROUP)


def _fft1(p4, w1k, wcc):
    b = p4.shape[0]
    return pl.pallas_call(
        _fft1_kernel,
        out_shape=jax.ShapeDtypeStruct((b, 2, DFT_N1, DFT_N2, FOURIER_WIDTH), F32),
        grid=(b, DFT_N2 // SUBLANES),
        in_specs=[
            _resident(w1k.shape),
            _resident(wcc.shape),
            pl.BlockSpec((1, DFT_N1, SUBLANES, FOURIER_WIDTH), lambda bi, j: (bi, 0, j, 0)),
        ],
        out_specs=pl.BlockSpec((1, 2, DFT_N1, SUBLANES, FOURIER_WIDTH),
                               lambda bi, j: (bi, 0, 0, j, 0)),
        compiler_params=pltpu.CompilerParams(
            dimension_semantics=("arbitrary", "arbitrary"), vmem_limit_bytes=VMEM_LIMIT),
        name="fft1",
    )(w1k, wcc, p4)


def _mix_out_kernel(x_ref, rest_ref, y_ref, gd_ref, g_ref, wga_ref, wgb_ref, wf_ref, wo_ref,
                    o_ref, m_ref, gb_ref):
    t = OUT_TOK

    @pl.when(pl.program_id(1) == 0)
    def _():
        row_j = lax.broadcasted_iota(jnp.int32, (t, LANES), 0) & (SUBLANES - 1)
        for ri in range(2):
            blk = gd_ref[0, :, ri * DFT_N2:(ri + 1) * DFT_N2].astype(F32)
            for jp in range(SUBLANES):
                c0 = ri * t + jp * DFT_N2
                gb_ref[:, c0:c0 + DFT_N2] = jnp.where(row_j == jp, blk, 0.0).astype(BF16)

    x = x_ref[0].reshape(t, D_MODEL)
    yy = y_ref[0].reshape(2 * t, FOURIER_WIDTH).astype(BF16)
    f = _dot(gb_ref[...], yy).astype(BF16)
    h = _rmsnorm_bf16(x, g_ref[...])
    rest = rest_ref[0].reshape(t, D_MODEL)
    for q in range(N_GROUPS):
        cols = slice(q * OUT_GROUP, (q + 1) * OUT_GROUP)
        wg_ref = wga_ref if q < GATE_BLK // OUT_GROUP else wgb_ref
        gc = (q * OUT_GROUP) % GATE_BLK
        gate_f = jax.nn.sigmoid(_dot(h, wg_ref[:, gc:gc + OUT_GROUP]))
        y_f = _dot(f[:, q * FOURIER_GROUP:(q + 1) * FOURIER_GROUP], wf_ref[q])
        m_ref[:, cols] = (gate_f * y_f + rest[:, cols]).astype(BF16)
    o = x + _dot(m_ref[...], wo_ref[...])
    o_ref[0] = o.reshape(DFT_N2, SUBLANES, D_MODEL)


def _mix_out(x4, rest4, y5, l, gd, g, w_in, wf, wo):
    b = x4.shape[0]
    d = D_MODEL
    tok_blk = (1, DFT_N2, SUBLANES, d)
    tok = lambda ki, bi: (bi, 0, ki, 0)
    gate_col = OFF_G // GATE_BLK
    return pl.pallas_call(
        _mix_out_kernel,
        out_shape=jax.ShapeDtypeStruct(x4.shape, F32),
        grid=(DFT_N1 // SUBLANES, b),
        in_specs=[
            pl.BlockSpec(tok_blk, tok),
            pl.BlockSpec(tok_blk, tok),
            pl.BlockSpec((1, 2, SUBLANES, DFT_N2, FOURIER_WIDTH), lambda ki, bi: (bi, 0, ki, 0, 0)),
            pl.BlockSpec((1, OUT_TOK, 2 * DFT_N2), lambda ki, bi: (ki, 0, 0)),
            _layer(l, (1, d)),
            _layer(l, (d, GATE_BLK), gate_col),
            _layer(l, (d, GATE_BLK), gate_col + 1),
            pl.BlockSpec((None, N_GROUPS, FOURIER_GROUP, OUT_GROUP), lambda *_: (l, 0, 0, 0),
                         pipeline_mode=pl.Buffered(1)),
            _layer(l, (d, d)),
        ],
        out_specs=pl.BlockSpec(tok_blk, tok),
        scratch_shapes=[pltpu.VMEM((OUT_TOK, d), BF16), pltpu.VMEM((OUT_TOK, 2 * OUT_TOK), BF16)],
        compiler_params=pltpu.CompilerParams(
            dimension_semantics=("arbitrary", "arbitrary"), vmem_limit_bytes=VMEM_LIMIT),
        name="mix_out",
    )(x4, rest4, y5, gd, g, w_in, w_in, wf, wo)


def _dft_tables():
    two_pi = 2.0 * np.pi
    r8 = SUBLANES
    c = np.arange(FOURIER_GROUP)
    ang = two_pi * ((c[:, None] * c[None, :]) % FOURIER_GROUP) / FOURIER_GROUP
    cs, sn = np.cos(ang), np.sin(ang)
    wcc = np.block([[cs, -sn], [sn, cs]])

    k1 = np.arange(DFT_N1)
    ang = two_pi * ((k1[:, None] * k1[None, :]) % DFT_N1) / DFT_N1
    w1 = np.stack([np.cos(ang), -np.sin(ang)], axis=0)
    eye = np.eye(r8)
    w1k = (w1[:, :, None, :, None] * eye[None, None, :, None, :]).reshape(2 * FFT1_ROWS, FFT1_ROWS)

    nkb = DFT_N1 // r8
    kk = (r8 * np.arange(nkb)[:, None, None] + DFT_N1 * np.arange(DFT_N2)[None, :, None]
          + np.arange(r8)[None, None, :])
    ang = two_pi * ((kk[..., None] * np.arange(DFT_N2)) % SEQ) / SEQ
    norm = 1.0 / math.sqrt(FOURIER_GROUP * SEQ)
    gd = (np.stack([np.cos(ang), np.sin(ang)], axis=3) * norm).reshape(nkb, OUT_TOK, 2 * DFT_N2)
    as_bf16 = lambda a: jnp.asarray(a.astype(BF16))
    return as_bf16(wcc), as_bf16(w1k), as_bf16(gd)


def kernel(x, g_ffn1, w1_a, w3_a, w2_a, g_mix, w_in, w_fourier, w_conv, w_conv_out,
           w_pool, pool_scale, w_o, g_ffn2, w1_b, w3_b, w2_b, g_final):
    b, s, d = x.shape
    assert (s, d) == (SEQ, D_MODEL)
    wcc, w1k, gd = _dft_tables()
    row = lambda v: v.reshape(DEPTH, 1, d)
    gfin = g_final.reshape(1, d)
    g_ffn1, g_mix, g_ffn2, pool_scale = row(g_ffn1), row(g_mix), row(g_ffn2), row(pool_scale)
    w1_a, w3_a, w2_a = w1_a.astype(BF16), w3_a.astype(BF16), w2_a.astype(BF16)
    w1_b, w3_b, w2_b = w1_b.astype(BF16), w3_b.astype(BF16), w2_b.astype(BF16)
    w_in, w_o = w_in.astype(BF16), w_o.astype(BF16)
    w_fourier, w_pool, w_conv_out = w_fourier.astype(BF16), w_pool.astype(BF16), w_conv_out.astype(BF16)
    view = (b, DFT_N2, DFT_N1, d)
    for l in range(DEPTH):
        x = _ffn(x.reshape(b * s, d), l, g_ffn1, w1_a, w3_a, w2_a, gfin, False).reshape(b, s, d)
        pf, rest = _mix_in(x, l, g_mix, w_in, w_conv, w_conv_out, w_pool, pool_scale)
        y = _fft1(pf.reshape(b, DFT_N1, DFT_N2, FOURIER_WIDTH), w1k, wcc)
        x = _mix_out(x.reshape(view), rest.reshape(view), y, l, gd, g_mix, w_in, w_fourier,
                     w_o).reshape(b, s, d)
        x = _ffn(x.reshape(b * s, d), l, g_ffn2, w1_b, w3_b, w2_b, gfin,
                 l == DEPTH - 1).reshape(b, s, d)
    return x
```

```python
import functools
import math

import numpy as np
import jax
import jax.numpy as jnp
from jax import lax
from jax.experimental import pallas as pl
from jax.experimental.pallas import tpu as pltpu

D_MODEL = 1024
SEQ = 8192
DEPTH = 4
N_GROUPS = 4
FOURIER_WIDTH = 512
FOURIER_GROUP = FOURIER_WIDTH // N_GROUPS
CONV_WIDTH = 512
POOL_WIDTH = 512
POOL_GROUP = POOL_WIDTH // N_GROUPS
POOL_WINDOWS = (2, 4, 8, 16)
OUT_GROUP = D_MODEL // N_GROUPS
D_FF = 2816
EPS = 1e-6

OFF_F = 0
OFF_B = OFF_F + FOURIER_WIDTH
OFF_C = OFF_B + CONV_WIDTH
OFF_V = OFF_C + CONV_WIDTH
OFF_P = OFF_V + CONV_WIDTH
OFF_G = OFF_P + POOL_WIDTH
IN_WIDTH = OFF_G + 3 * D_MODEL

SUBLANES = 8
LANES = 128
MXU_COLS = 256
VMEM_LIMIT = 56 * 1024 * 1024

DFT_N1 = 64
DFT_N2 = SEQ // DFT_N1
FFT1_ROWS = DFT_N1 * SUBLANES
OUT_TOK = DFT_N2 * SUBLANES
GATE_BLK = 512

HALO = 8
FFN_TM = 1024
FFN_SUB = 512
FFT1_SUB = 4
MIX_TM = 512

BF16 = jnp.bfloat16
F32 = jnp.float32


def _dot(a, b):
    return jnp.dot(a, b, preferred_element_type=F32)


def _rmsnorm_bf16(x, g):
    ms = jnp.mean(x * x, axis=-1, keepdims=True)
    return (x * lax.rsqrt(ms + EPS) * g).astype(BF16)


def _resident(shape):
    nd = len(shape)
    return pl.BlockSpec(shape, lambda *_: (0,) * nd, pipeline_mode=pl.Buffered(1))


def _layer(l, shape, col=0):
    return pl.BlockSpec((None,) + shape, lambda *_: (l, 0, col), pipeline_mode=pl.Buffered(1))


def _ffn_kernel(x_ref, g_ref, w1_ref, w3_ref, w2_ref, gfin_ref, o_ref, act_ref, *, final_norm):
    for r0 in range(0, FFN_TM, FFN_SUB):
        rows = slice(r0, r0 + FFN_SUB)
        x = x_ref[rows, :]
        h = _rmsnorm_bf16(x, g_ref[...])
        for c in range(0, D_FF, MXU_COLS):
            a = _dot(h, w1_ref[:, c:c + MXU_COLS])
            b = _dot(h, w3_ref[:, c:c + MXU_COLS])
            act_ref[rows, c:c + MXU_COLS] = (jax.nn.silu(a) * b).astype(BF16)
        y = x + 0.5 * _dot(act_ref[rows, :], w2_ref[...])
        if final_norm:
            ms = jnp.mean(y * y, axis=-1, keepdims=True)
            y = y * lax.rsqrt(ms + EPS) * gfin_ref[...]
        o_ref[rows, :] = y


def _ffn(x2d, l, g, w1, w3, w2, gfin, final_norm):
    t, d = x2d.shape
    tm = FFN_TM
    return pl.pallas_call(
        functools.partial(_ffn_kernel, final_norm=final_norm),
        out_shape=jax.ShapeDtypeStruct((t, d), F32),
        grid=(t // tm,),
        in_specs=[
            pl.BlockSpec((tm, d), lambda i: (i, 0)),
            _layer(l, (1, d)),
            _layer(l, (d, D_FF)),
            _layer(l, (d, D_FF)),
            _layer(l, (D_FF, d)),
            _resident((1, d)),
        ],
        out_specs=pl.BlockSpec((tm, d), lambda i: (i, 0)),
        scratch_shapes=[pltpu.VMEM((tm, D_FF), BF16)],
        compiler_params=pltpu.CompilerParams(
            dimension_semantics=("arbitrary",), vmem_limit_bytes=VMEM_LIMIT),
        name="ffn_final" if final_norm else "ffn",
    )(x2d, g, w1, w3, w2, gfin)


def _mix_in_kernel(x_ref, xp_ref, xn_ref, g_ref, win_ref, wconv_ref, wco_ref,
                   wpool_ref, pscale_ref, pf_ref, rest_ref, ext_ref):
    tm = MIX_TM
    i = pl.program_id(1)
    nt = pl.num_programs(1)
    g = g_ref[...]
    h = _rmsnorm_bf16(x_ref[0], g)
    hh = _rmsnorm_bf16(jnp.concatenate([xp_ref[0], xn_ref[0]], axis=0), g)

    pf_ref[0] = _dot(h, win_ref[:, OFF_F:OFF_B])

    ph = _dot(hh, win_ref[:, OFF_C:OFF_G])
    halo = jnp.concatenate(
        [ph[:, 0:CONV_WIDTH] * ph[:, CONV_WIDTH:2 * CONV_WIDTH], ph[:, 2 * CONV_WIDTH:]], axis=1)
    ext_ref[0:HALO, :] = jnp.where(i > 0, halo[0:HALO], 0.0)
    ext_ref[HALO + tm:, :] = jnp.where(i < nt - 1, halo[HALO:], 0.0)
    zc = _dot(h, win_ref[:, OFF_C:OFF_V]) * _dot(h, win_ref[:, OFF_V:OFF_P])
    ext_ref[HALO:HALO + tm, 0:CONV_WIDTH] = zc
    ext_ref[HALO:HALO + tm, CONV_WIDTH:] = _dot(h, win_ref[:, OFF_P:OFF_G])

    wconv = wconv_ref[...]
    conv = (wconv[0:1] * ext_ref[HALO - 1:HALO - 1 + tm, 0:CONV_WIDTH]
            + wconv[1:2] * zc
            + wconv[2:3] * ext_ref[HALO + 1:HALO + 1 + tm, 0:CONV_WIDTH])
    bconv = (_dot(h, win_ref[:, OFF_B:OFF_C]) * conv).astype(BF16)

    t = (i * tm + lax.broadcasted_iota(jnp.int32, (tm, 1), 0)).astype(F32)
    pscale = pscale_ref[...]
    for q in range(N_GROUPS):
        half = POOL_WINDOWS[q] // 2
        lo = CONV_WIDTH + q * POOL_GROUP
        win = ext_ref[HALO - half:HALO - half + tm, lo:lo + POOL_GROUP]
        for j in range(1 - half, half):
            win = win + ext_ref[HALO + j:HALO + j + tm, lo:lo + POOL_GROUP]
        count = jnp.minimum(t + half, float(SEQ)) - jnp.maximum(t - half, 0.0)
        pooled = (win / count - ext_ref[HALO:HALO + tm, lo:lo + POOL_GROUP]).astype(BF16)

        cols = slice(q * OUT_GROUP, (q + 1) * OUT_GROUP)
        gcol = OFF_G + D_MODEL + q * OUT_GROUP
        gate_c = jax.nn.sigmoid(_dot(h, win_ref[:, gcol:gcol + OUT_GROUP]))
        gate_p = jax.nn.sigmoid(_dot(h, win_ref[:, gcol + D_MODEL:gcol + D_MODEL + OUT_GROUP]))
        y_c = _dot(bconv, wco_ref[:, cols])
        y_p = _dot(pooled, wpool_ref[q]) * pscale[:, cols]
        rest_ref[0, :, cols] = gate_c * y_c + gate_p * y_p


def _mix_in(x, l, g, w_in, w_conv, w_co, w_pool, pscale):
    b, s, d = x.shape
    tm = MIX_TM
    nt = s // tm
    hb = tm // HALO
    last_hb = s // HALO - 1
    return pl.pallas_call(
        _mix_in_kernel,
        out_shape=(
            jax.ShapeDtypeStruct((b, s, FOURIER_WIDTH), F32),
            jax.ShapeDtypeStruct((b, s, d), F32),
        ),
        grid=(b, nt),
        in_specs=[
            pl.BlockSpec((1, tm, d), lambda bi, i: (bi, i, 0)),
            pl.BlockSpec((1, HALO, d), lambda bi, i: (bi, jnp.maximum(i * hb - 1, 0), 0)),
            pl.BlockSpec((1, HALO, d), lambda bi, i: (bi, jnp.minimum((i + 1) * hb, last_hb), 0)),
            _layer(l, (1, d)),
            _layer(l, (d, IN_WIDTH)),
            _layer(l, (3, CONV_WIDTH)),
            _layer(l, (CONV_WIDTH, d)),
            pl.BlockSpec((None, N_GROUPS, POOL_GROUP, OUT_GROUP), lambda *_: (l, 0, 0, 0),
                         pipeline_mode=pl.Buffered(1)),
            _layer(l, (1, d)),
        ],
        out_specs=(
            pl.BlockSpec((1, tm, FOURIER_WIDTH), lambda bi, i: (bi, i, 0)),
            pl.BlockSpec((1, tm, d), lambda bi, i: (bi, i, 0)),
        ),
        scratch_shapes=[pltpu.VMEM((tm + 2 * HALO, CONV_WIDTH + POOL_WIDTH), F32)],
        compiler_params=pltpu.CompilerParams(
            dimension_semantics=("arbitrary", "arbitrary"), vmem_limit_bytes=VMEM_LIMIT),
        name="mix_in",
    )(x, x, x, g, w_in, w_conv, w_co, w_pool, pscale)


def _fft1_kernel(w_ref, wcc_ref, p_ref, o_ref):
    rows = FFT1_ROWS
    for s in range(FFT1_SUB):
        r8 = slice(s * SUBLANES, (s + 1) * SUBLANES)
        pf = p_ref[0, :, r8, :].reshape(rows, FOURIER_WIDTH).astype(BF16)
        a = _dot(w_ref[...], pf).astype(BF16)
        for q in range(N_GROUPS):
            cols = slice(q * FOURIER_GROUP, (q + 1) * FOURIER_GROUP)
            y = _dot(jnp.concatenate([a[:rows, cols], a[rows:, cols]], axis=1), wcc_ref[...])
            o_ref[0, 0, :, r8, cols] = y[:, :FOURIER_GROUP].reshape(DFT_N1, SUBLANES, FOURIER_GROUP)
            o_ref[0, 1, :, r8, cols] = y[:, FOURIER_GROUP:].reshape(DFT_N1, SUBLANES, FOURIER_GROUP)


def _fft1(p4, w1k, wcc):
    b = p4.shape[0]
    n2_blk = FFT1_SUB * SUBLANES
    return pl.pallas_call(
        _fft1_kernel,
        out_shape=jax.ShapeDtypeStruct((b, 2, DFT_N1, DFT_N2, FOURIER_WIDTH), F32),
        grid=(b, DFT_N2 // n2_blk),
        in_specs=[
            _resident(w1k.shape),
            _resident(wcc.shape),
            pl.BlockSpec((1, DFT_N1, n2_blk, FOURIER_WIDTH), lambda bi, j: (bi, 0, j, 0)),
        ],
        out_specs=pl.BlockSpec((1, 2, DFT_N1, n2_blk, FOURIER_WIDTH),
                               lambda bi, j: (bi, 0, 0, j, 0)),
        compiler_params=pltpu.CompilerParams(
            dimension_semantics=("arbitrary", "arbitrary"), vmem_limit_bytes=VMEM_LIMIT),
        name="fft1",
    )(w1k, wcc, p4)


def _mix_out_kernel(x_ref, rest_ref, y_ref, gd_ref, g_ref, wga_ref, wgb_ref, wf_ref, wo_ref,
                    o_ref, m_ref, f_ref):
    t = OUT_TOK
    for j in range(SUBLANES):
        yj = jnp.concatenate([y_ref[0, 0, j], y_ref[0, 1, j]], axis=0).astype(BF16)
        fj = _dot(gd_ref[0, j], yj)
        for c in range(FOURIER_WIDTH // LANES):
            f_ref[c, pl.ds(j, DFT_N2, stride=SUBLANES), :] = fj[:, c * LANES:(c + 1) * LANES]

    x = x_ref[0].reshape(t, D_MODEL)
    h = _rmsnorm_bf16(x, g_ref[...])
    rest = rest_ref[0].reshape(t, D_MODEL)
    for q in range(N_GROUPS):
        cols = slice(q * OUT_GROUP, (q + 1) * OUT_GROUP)
        wg_ref = wga_ref if q < GATE_BLK // OUT_GROUP else wgb_ref
        gc = (q * OUT_GROUP) % GATE_BLK
        gate_f = jax.nn.sigmoid(_dot(h, wg_ref[:, gc:gc + OUT_GROUP]))
        y_f = _dot(f_ref[q].astype(BF16), wf_ref[q])
        m_ref[:, cols] = (gate_f * y_f + rest[:, cols]).astype(BF16)
    o = x + _dot(m_ref[...], wo_ref[...])
    o_ref[0] = o.reshape(DFT_N2, SUBLANES, D_MODEL)


def _mix_out(x4, rest4, y5, l, gd, g, w_in, wf, wo):
    b = x4.shape[0]
    d = D_MODEL
    tok_blk = (1, DFT_N2, SUBLANES, d)
    tok = lambda ki, bi: (bi, 0, ki, 0)
    gate_col = OFF_G // GATE_BLK
    return pl.pallas_call(
        _mix_out_kernel,
        out_shape=jax.ShapeDtypeStruct(x4.shape, F32),
        grid=(DFT_N1 // SUBLANES, b),
        in_specs=[
            pl.BlockSpec(tok_blk, tok),
            pl.BlockSpec(tok_blk, tok),
            pl.BlockSpec((1, 2, SUBLANES, DFT_N2, FOURIER_WIDTH), lambda ki, bi: (bi, 0, ki, 0, 0)),
            pl.BlockSpec((1, SUBLANES, DFT_N2, 2 * DFT_N2), lambda ki, bi: (ki, 0, 0, 0)),
            _layer(l, (1, d)),
            _layer(l, (d, GATE_BLK), gate_col),
            _layer(l, (d, GATE_BLK), gate_col + 1),
            pl.BlockSpec((None, N_GROUPS, FOURIER_GROUP, OUT_GROUP), lambda *_: (l, 0, 0, 0),
                         pipeline_mode=pl.Buffered(1)),
            _layer(l, (d, d)),
        ],
        out_specs=pl.BlockSpec(tok_blk, tok),
        scratch_shapes=[pltpu.VMEM((OUT_TOK, d), BF16),
                        pltpu.VMEM((FOURIER_WIDTH // LANES, OUT_TOK, LANES), F32)],
        compiler_params=pltpu.CompilerParams(
            dimension_semantics=("arbitrary", "arbitrary"), vmem_limit_bytes=VMEM_LIMIT),
        name="mix_out",
    )(x4, rest4, y5, gd, g, w_in, w_in, wf, wo)


def _dft_tables():
    two_pi = 2.0 * np.pi
    r8 = SUBLANES
    c = np.arange(FOURIER_GROUP)
    ang = two_pi * ((c[:, None] * c[None, :]) % FOURIER_GROUP) / FOURIER_GROUP
    cs, sn = np.cos(ang), np.sin(ang)
    wcc = np.block([[cs, -sn], [sn, cs]])

    k1 = np.arange(DFT_N1)
    ang = two_pi * ((k1[:, None] * k1[None, :]) % DFT_N1) / DFT_N1
    w1 = np.stack([np.cos(ang), -np.sin(ang)], axis=0)
    eye = np.eye(r8)
    w1k = (w1[:, :, None, :, None] * eye[None, None, :, None, :]).reshape(2 * FFT1_ROWS, FFT1_ROWS)

    nkb = DFT_N1 // r8
    kk = (r8 * np.arange(nkb)[:, None, None] + np.arange(r8)[None, :, None]
          + DFT_N1 * np.arange(DFT_N2)[None, None, :])
    ang = two_pi * ((kk[..., None] * np.arange(DFT_N2)) % SEQ) / SEQ
    norm = 1.0 / math.sqrt(FOURIER_GROUP * SEQ)
    gd = (np.stack([np.cos(ang), np.sin(ang)], axis=3) * norm).reshape(nkb, r8, DFT_N2, 2 * DFT_N2)
    as_bf16 = lambda a: jnp.asarray(a.astype(BF16))
    return as_bf16(wcc), as_bf16(w1k), as_bf16(gd)


def kernel(x, g_ffn1, w1_a, w3_a, w2_a, g_mix, w_in, w_fourier, w_conv, w_conv_out,
           w_pool, pool_scale, w_o, g_ffn2, w1_b, w3_b, w2_b, g_final):
    b, s, d = x.shape
    assert (s, d) == (SEQ, D_MODEL)
    wcc, w1k, gd = _dft_tables()
    row = lambda v: v.reshape(DEPTH, 1, d)
    gfin = g_final.reshape(1, d)
    g_ffn1, g_mix, g_ffn2, pool_scale = row(g_ffn1), row(g_mix), row(g_ffn2), row(pool_scale)
    w1_a, w3_a, w2_a = w1_a.astype(BF16), w3_a.astype(BF16), w2_a.astype(BF16)
    w1_b, w3_b, w2_b = w1_b.astype(BF16), w3_b.astype(BF16), w2_b.astype(BF16)
    w_in, w_o = w_in.astype(BF16), w_o.astype(BF16)
    w_fourier, w_pool, w_conv_out = w_fourier.astype(BF16), w_pool.astype(BF16), w_conv_out.astype(BF16)
    view = (b, DFT_N2, DFT_N1, d)
    for l in range(DEPTH):
        x = _ffn(x.reshape(b * s, d), l, g_ffn1, w1_a, w3_a, w2_a, gfin, False).reshape(b, s, d)
        pf, rest = _mix_in(x, l, g_mix, w_in, w_conv, w_conv_out, w_pool, pool_scale)
        y = _fft1(pf.reshape(b, DFT_N1, DFT_N2, FOURIER_WIDTH), w1k, wcc)
        x = _mix_out(x.reshape(view), rest.reshape(view), y, l, gd, g_mix, w_in, w_fourier,
                     w_o).reshape(b, s, d)
        x = _ffn(x.reshape(b * s, d), l, g_ffn2, w1_b, w3_b, w2_b, gfin,
                 l == DEPTH - 1).reshape(b, s, d)
    return x
```

```python
import functools
import math

import numpy as np
import jax
import jax.numpy as jnp
from jax import lax
from jax.experimental import pallas as pl
from jax.experimental.pallas import tpu as pltpu

D_MODEL = 1024
SEQ = 8192
DEPTH = 4
N_GROUPS = 4
FOURIER_WIDTH = 512
FOURIER_GROUP = FOURIER_WIDTH // N_GROUPS
CONV_WIDTH = 512
POOL_WIDTH = 512
POOL_GROUP = POOL_WIDTH // N_GROUPS
POOL_WINDOWS = (2, 4, 8, 16)
OUT_GROUP = D_MODEL // N_GROUPS
D_FF = 2816
EPS = 1e-6

OFF_F = 0
OFF_B = OFF_F + FOURIER_WIDTH
OFF_C = OFF_B + CONV_WIDTH
OFF_V = OFF_C + CONV_WIDTH
OFF_P = OFF_V + CONV_WIDTH
OFF_G = OFF_P + POOL_WIDTH
IN_WIDTH = OFF_G + 3 * D_MODEL

SUBLANES = 8
LANES = 128
MXU_COLS = 256
VMEM_LIMIT = 56 * 1024 * 1024

DFT_N1 = 64
DFT_N2 = SEQ // DFT_N1
FFT1_ROWS = DFT_N1 * SUBLANES
OUT_TOK = DFT_N2 * SUBLANES
GATE_BLK = 512

HALO = 8
FFN_TM = 1024
FFN_SUB = 256
FFT1_SUB = 4
MIX_TM = 1024
MIX_SUB = 512

BF16 = jnp.bfloat16
F32 = jnp.float32


def _dot(a, b):
    return jnp.dot(a, b, preferred_element_type=F32)


def _rmsnorm_bf16(x, g):
    ms = jnp.mean(x * x, axis=-1, keepdims=True)
    return (x * lax.rsqrt(ms + EPS) * g).astype(BF16)


def _resident(shape):
    nd = len(shape)
    return pl.BlockSpec(shape, lambda *_: (0,) * nd, pipeline_mode=pl.Buffered(1))


def _layer(l, shape, col=0):
    return pl.BlockSpec((None,) + shape, lambda *_: (l, 0, col), pipeline_mode=pl.Buffered(1))


def _ffn_kernel(x_ref, g_ref, w1_ref, w3_ref, w2_ref, gfin_ref, o_ref, act_ref, *, final_norm):
    for r0 in range(0, FFN_TM, FFN_SUB):
        rows = slice(r0, r0 + FFN_SUB)
        x = x_ref[rows, :]
        h = _rmsnorm_bf16(x, g_ref[...])
        for c in range(0, D_FF, MXU_COLS):
            a = _dot(h, w1_ref[:, c:c + MXU_COLS])
            b = _dot(h, w3_ref[:, c:c + MXU_COLS])
            act_ref[rows, c:c + MXU_COLS] = (jax.nn.silu(a) * b).astype(BF16)
        y = x + 0.5 * _dot(act_ref[rows, :], w2_ref[...])
        if final_norm:
            ms = jnp.mean(y * y, axis=-1, keepdims=True)
            y = y * lax.rsqrt(ms + EPS) * gfin_ref[...]
        o_ref[rows, :] = y


def _ffn(x2d, l, g, w1, w3, w2, gfin, final_norm):
    t, d = x2d.shape
    tm = FFN_TM
    return pl.pallas_call(
        functools.partial(_ffn_kernel, final_norm=final_norm),
        out_shape=jax.ShapeDtypeStruct((t, d), F32),
        grid=(t // tm,),
        in_specs=[
            pl.BlockSpec((tm, d), lambda i: (i, 0)),
            _layer(l, (1, d)),
            _layer(l, (d, D_FF)),
            _layer(l, (d, D_FF)),
            _layer(l, (D_FF, d)),
            _resident((1, d)),
        ],
        out_specs=pl.BlockSpec((tm, d), lambda i: (i, 0)),
        scratch_shapes=[pltpu.VMEM((tm, D_FF), BF16)],
        compiler_params=pltpu.CompilerParams(
            dimension_semantics=("arbitrary",), vmem_limit_bytes=VMEM_LIMIT),
        name="ffn_final" if final_norm else "ffn",
    )(x2d, g, w1, w3, w2, gfin)


def _mix_in_kernel(x_ref, xp_ref, xn_ref, g_ref, win_ref, wconv_ref, wco_ref,
                   wpool_ref, pscale_ref, pf_ref, rest_ref, ext_ref, h_ref):
    tm = MIX_TM
    sub = MIX_SUB
    i = pl.program_id(1)
    nt = pl.num_programs(1)
    g = g_ref[...]

    hh = _rmsnorm_bf16(jnp.concatenate([xp_ref[0], xn_ref[0]], axis=0), g)
    ph = _dot(hh, win_ref[:, OFF_C:OFF_G])
    halo = jnp.concatenate(
        [ph[:, 0:CONV_WIDTH] * ph[:, CONV_WIDTH:2 * CONV_WIDTH], ph[:, 2 * CONV_WIDTH:]], axis=1)
    ext_ref[0:HALO, :] = jnp.where(i > 0, halo[0:HALO], 0.0)
    ext_ref[HALO + tm:, :] = jnp.where(i < nt - 1, halo[HALO:], 0.0)

    for r0 in range(0, tm, sub):
        rows = slice(r0, r0 + sub)
        erows = slice(HALO + r0, HALO + r0 + sub)
        h = _rmsnorm_bf16(x_ref[0, rows, :], g)
        h_ref[rows, :] = h
        pf_ref[0, rows, :] = _dot(h, win_ref[:, OFF_F:OFF_B]).astype(BF16)
        ext_ref[erows, 0:CONV_WIDTH] = (_dot(h, win_ref[:, OFF_C:OFF_V])
                                        * _dot(h, win_ref[:, OFF_V:OFF_P]))
        ext_ref[erows, CONV_WIDTH:] = _dot(h, win_ref[:, OFF_P:OFF_G])

    wconv = wconv_ref[...]
    pscale = pscale_ref[...]
    for r0 in range(0, tm, sub):
        rows = slice(r0, r0 + sub)
        e0 = HALO + r0
        h = h_ref[rows, :]
        conv = (wconv[0:1] * ext_ref[e0 - 1:e0 - 1 + sub, 0:CONV_WIDTH]
                + wconv[1:2] * ext_ref[e0:e0 + sub, 0:CONV_WIDTH]
                + wconv[2:3] * ext_ref[e0 + 1:e0 + 1 + sub, 0:CONV_WIDTH])
        bconv = (_dot(h, win_ref[:, OFF_B:OFF_C]) * conv).astype(BF16)

        t = (i * tm + r0 + lax.broadcasted_iota(jnp.int32, (sub, 1), 0)).astype(F32)
        for q in range(N_GROUPS):
            half = POOL_WINDOWS[q] // 2
            lo = CONV_WIDTH + q * POOL_GROUP
            win = ext_ref[e0 - half:e0 - half + sub, lo:lo + POOL_GROUP]
            for j in range(1 - half, half):
                win = win + ext_ref[e0 + j:e0 + j + sub, lo:lo + POOL_GROUP]
            count = jnp.minimum(t + half, float(SEQ)) - jnp.maximum(t - half, 0.0)
            pooled = (win / count - ext_ref[e0:e0 + sub, lo:lo + POOL_GROUP]).astype(BF16)

            cols = slice(q * OUT_GROUP, (q + 1) * OUT_GROUP)
            gcol = OFF_G + D_MODEL + q * OUT_GROUP
            gate_c = jax.nn.sigmoid(_dot(h, win_ref[:, gcol:gcol + OUT_GROUP]))
            gate_p = jax.nn.sigmoid(_dot(h, win_ref[:, gcol + D_MODEL:gcol + D_MODEL + OUT_GROUP]))
            y_c = _dot(bconv, wco_ref[:, cols])
            y_p = _dot(pooled, wpool_ref[q]) * pscale[:, cols]
            rest_ref[0, rows, cols] = gate_c * y_c + gate_p * y_p


def _mix_in(x, l, g, w_in, w_conv, w_co, w_pool, pscale):
    b, s, d = x.shape
    tm = MIX_TM
    nt = s // tm
    hb = tm // HALO
    last_hb = s // HALO - 1
    return pl.pallas_call(
        _mix_in_kernel,
        out_shape=(
            jax.ShapeDtypeStruct((b, s, FOURIER_WIDTH), BF16),
            jax.ShapeDtypeStruct((b, s, d), F32),
        ),
        grid=(b, nt),
        in_specs=[
            pl.BlockSpec((1, tm, d), lambda bi, i: (bi, i, 0)),
            pl.BlockSpec((1, HALO, d), lambda bi, i: (bi, jnp.maximum(i * hb - 1, 0), 0)),
            pl.BlockSpec((1, HALO, d), lambda bi, i: (bi, jnp.minimum((i + 1) * hb, last_hb), 0)),
            _layer(l, (1, d)),
            _layer(l, (d, IN_WIDTH)),
            _layer(l, (3, CONV_WIDTH)),
            _layer(l, (CONV_WIDTH, d)),
            pl.BlockSpec((None, N_GROUPS, POOL_GROUP, OUT_GROUP), lambda *_: (l, 0, 0, 0),
                         pipeline_mode=pl.Buffered(1)),
            _layer(l, (1, d)),
        ],
        out_specs=(
            pl.BlockSpec((1, tm, FOURIER_WIDTH), lambda bi, i: (bi, i, 0)),
            pl.BlockSpec((1, tm, d), lambda bi, i: (bi, i, 0)),
        ),
        scratch_shapes=[pltpu.VMEM((tm + 2 * HALO, CONV_WIDTH + POOL_WIDTH), F32),
                        pltpu.VMEM((tm, d), BF16)],
        compiler_params=pltpu.CompilerParams(
            dimension_semantics=("arbitrary", "arbitrary"), vmem_limit_bytes=VMEM_LIMIT),
        name="mix_in",
    )(x, x, x, g, w_in, w_conv, w_co, w_pool, pscale)


def _fft1_kernel(w_ref, wcc_ref, p_ref, o_ref):
    rows = FFT1_ROWS
    p = p_ref[0].astype(F32)
    a = []
    for s in range(FFT1_SUB):
        pf = p[:, s * SUBLANES:(s + 1) * SUBLANES, :].reshape(rows, FOURIER_WIDTH).astype(BF16)
        a.append(_dot(w_ref[...], pf).astype(BF16))
    tile = (DFT_N1, SUBLANES, FOURIER_GROUP)
    for q in range(N_GROUPS):
        cols = slice(q * FOURIER_GROUP, (q + 1) * FOURIER_GROUP)
        y = [_dot(jnp.concatenate([a_s[:rows, cols], a_s[rows:, cols]], axis=1), wcc_ref[...])
             for a_s in a]
        o_ref[0, 0, :, :, cols] = jnp.concatenate(
            [y_s[:, :FOURIER_GROUP].reshape(tile) for y_s in y], axis=1).astype(BF16)
        o_ref[0, 1, :, :, cols] = jnp.concatenate(
            [y_s[:, FOURIER_GROUP:].reshape(tile) for y_s in y], axis=1).astype(BF16)


def _fft1(p4, w1k, wcc):
    b = p4.shape[0]
    n2_blk = FFT1_SUB * SUBLANES
    return pl.pallas_call(
        _fft1_kernel,
        out_shape=jax.ShapeDtypeStruct((b, 2, DFT_N1, DFT_N2, FOURIER_WIDTH), BF16),
        grid=(b, DFT_N2 // n2_blk),
        in_specs=[
            _resident(w1k.shape),
            _resident(wcc.shape),
            pl.BlockSpec((1, DFT_N1, n2_blk, FOURIER_WIDTH), lambda bi, j: (bi, 0, j, 0)),
        ],
        out_specs=pl.BlockSpec((1, 2, DFT_N1, n2_blk, FOURIER_WIDTH),
                               lambda bi, j: (bi, 0, 0, j, 0)),
        compiler_params=pltpu.CompilerParams(
            dimension_semantics=("arbitrary", "arbitrary"), vmem_limit_bytes=VMEM_LIMIT),
        name="fft1",
    )(w1k, wcc, p4)


def _mix_out_kernel(x_ref, rest_ref, y_ref, gd_ref, g_ref, wga_ref, wgb_ref, wf_ref, wo_ref,
                    o_ref, m_ref, f_ref):
    t = OUT_TOK
    for j in range(SUBLANES):
        yj = jnp.concatenate([y_ref[0, 0, j], y_ref[0, 1, j]], axis=0)
        fj = _dot(gd_ref[0, j], yj)
        for c in range(FOURIER_WIDTH // LANES):
            f_ref[c, pl.ds(j, DFT_N2, stride=SUBLANES), :] = fj[:, c * LANES:(c + 1) * LANES]

    x = x_ref[0].reshape(t, D_MODEL)
    h = _rmsnorm_bf16(x, g_ref[...])
    rest = rest_ref[0].reshape(t, D_MODEL)
    for q in range(N_GROUPS):
        cols = slice(q * OUT_GROUP, (q + 1) * OUT_GROUP)
        wg_ref = wga_ref if q < GATE_BLK // OUT_GROUP else wgb_ref
        gc = (q * OUT_GROUP) % GATE_BLK
        gate_f = jax.nn.sigmoid(_dot(h, wg_ref[:, gc:gc + OUT_GROUP]))
        y_f = _dot(f_ref[q].astype(BF16), wf_ref[q])
        m_ref[:, cols] = (gate_f * y_f + rest[:, cols]).astype(BF16)
    o = x + _dot(m_ref[...], wo_ref[...])
    o_ref[0] = o.reshape(DFT_N2, SUBLANES, D_MODEL)


def _mix_out(x4, rest4, y5, l, gd, g, w_in, wf, wo):
    b = x4.shape[0]
    d = D_MODEL
    tok_blk = (1, DFT_N2, SUBLANES, d)
    tok = lambda ki, bi: (bi, 0, ki, 0)
    gate_col = OFF_G // GATE_BLK
    return pl.pallas_call(
        _mix_out_kernel,
        out_shape=jax.ShapeDtypeStruct(x4.shape, F32),
        grid=(DFT_N1 // SUBLANES, b),
        in_specs=[
            pl.BlockSpec(tok_blk, tok),
            pl.BlockSpec(tok_blk, tok),
            pl.BlockSpec((1, 2, SUBLANES, DFT_N2, FOURIER_WIDTH), lambda ki, bi: (bi, 0, ki, 0, 0)),
            pl.BlockSpec((1, SUBLANES, DFT_N2, 2 * DFT_N2), lambda ki, bi: (ki, 0, 0, 0)),
            _layer(l, (1, d)),
            _layer(l, (d, GATE_BLK), gate_col),
            _layer(l, (d, GATE_BLK), gate_col + 1),
            pl.BlockSpec((None, N_GROUPS, FOURIER_GROUP, OUT_GROUP), lambda *_: (l, 0, 0, 0),
                         pipeline_mode=pl.Buffered(1)),
            _layer(l, (d, d)),
        ],
        out_specs=pl.BlockSpec(tok_blk, tok),
        scratch_shapes=[pltpu.VMEM((OUT_TOK, d), BF16),
                        pltpu.VMEM((FOURIER_WIDTH // LANES, OUT_TOK, LANES), F32)],
        compiler_params=pltpu.CompilerParams(
            dimension_semantics=("arbitrary", "arbitrary"), vmem_limit_bytes=VMEM_LIMIT),
        name="mix_out",
    )(x4, rest4, y5, gd, g, w_in, w_in, wf, wo)


def _dft_tables():
    two_pi = 2.0 * np.pi
    r8 = SUBLANES
    c = np.arange(FOURIER_GROUP)
    ang = two_pi * ((c[:, None] * c[None, :]) % FOURIER_GROUP) / FOURIER_GROUP
    cs, sn = np.cos(ang), np.sin(ang)
    wcc = np.block([[cs, -sn], [sn, cs]])

    k1 = np.arange(DFT_N1)
    ang = two_pi * ((k1[:, None] * k1[None, :]) % DFT_N1) / DFT_N1
    w1 = np.stack([np.cos(ang), -np.sin(ang)], axis=0)
    eye = np.eye(r8)
    w1k = (w1[:, :, None, :, None] * eye[None, None, :, None, :]).reshape(2 * FFT1_ROWS, FFT1_ROWS)

    nkb = DFT_N1 // r8
    kk = (r8 * np.arange(nkb)[:, None, None] + np.arange(r8)[None, :, None]
          + DFT_N1 * np.arange(DFT_N2)[None, None, :])
    ang = two_pi * ((kk[..., None] * np.arange(DFT_N2)) % SEQ) / SEQ
    norm = 1.0 / math.sqrt(FOURIER_GROUP * SEQ)
    gd = (np.stack([np.cos(ang), np.sin(ang)], axis=3) * norm).reshape(nkb, r8, DFT_N2, 2 * DFT_N2)
    as_bf16 = lambda a: jnp.asarray(a.astype(BF16))
    return as_bf16(wcc), as_bf16(w1k), as_bf16(gd)


def kernel(x, g_ffn1, w1_a, w3_a, w2_a, g_mix, w_in, w_fourier, w_conv, w_conv_out,
           w_pool, pool_scale, w_o, g_ffn2, w1_b, w3_b, w2_b, g_final):
    b, s, d = x.shape
    assert (s, d) == (SEQ, D_MODEL)
    wcc, w1k, gd = _dft_tables()
    row = lambda v: v.reshape(DEPTH, 1, d)
    gfin = g_final.reshape(1, d)
    g_ffn1, g_mix, g_ffn2, pool_scale = row(g_ffn1), row(g_mix), row(g_ffn2), row(pool_scale)
    w1_a, w3_a, w2_a = w1_a.astype(BF16), w3_a.astype(BF16), w2_a.astype(BF16)
    w1_b, w3_b, w2_b = w1_b.astype(BF16), w3_b.astype(BF16), w2_b.astype(BF16)
    w_in, w_o = w_in.astype(BF16), w_o.astype(BF16)
    w_fourier, w_pool, w_conv_out = w_fourier.astype(BF16), w_pool.astype(BF16), w_conv_out.astype(BF16)
    view = (b, DFT_N2, DFT_N1, d)
    for l in range(DEPTH):
        x = _ffn(x.reshape(b * s, d), l, g_ffn1, w1_a, w3_a, w2_a, gfin, False).reshape(b, s, d)
        pf, rest = _mix_in(x, l, g_mix, w_in, w_conv, w_conv_out, w_pool, pool_scale)
        y = _fft1(pf.reshape(b, DFT_N1, DFT_N2, FOURIER_WIDTH), w1k, wcc)
        x = _mix_out(x.reshape(view), rest.reshape(view), y, l, gd, g_mix, w_in, w_fourier,
                     w_o).reshape(b, s, d)
        x = _ffn(x.reshape(b * s, d), l, g_ffn2, w1_b, w3_b, w2_b, gfin,
                 l == DEPTH - 1).reshape(b, s, d)
    return x
```

```python
import functools
import math

import numpy as np
import jax
import jax.numpy as jnp
from jax import lax
from jax.experimental import pallas as pl
from jax.experimental.pallas import tpu as pltpu

D_MODEL = 1024
SEQ = 8192
DEPTH = 4
N_GROUPS = 4
FOURIER_WIDTH = 512
FOURIER_GROUP = FOURIER_WIDTH // N_GROUPS
CONV_WIDTH = 512
POOL_WIDTH = 512
POOL_GROUP = POOL_WIDTH // N_GROUPS
POOL_WINDOWS = (2, 4, 8, 16)
OUT_GROUP = D_MODEL // N_GROUPS
D_FF = 2816
EPS = 1e-6

OFF_F = 0
OFF_B = OFF_F + FOURIER_WIDTH
OFF_C = OFF_B + CONV_WIDTH
OFF_V = OFF_C + CONV_WIDTH
OFF_P = OFF_V + CONV_WIDTH
OFF_G = OFF_P + POOL_WIDTH
IN_WIDTH = OFF_G + 3 * D_MODEL

SUBLANES = 8
LANES = 128
MXU_COLS = 256
VMEM_LIMIT = 56 * 1024 * 1024

DFT_N1 = 64
DFT_N2 = SEQ // DFT_N1
FFT1_ROWS = DFT_N1 * SUBLANES
OUT_TOK = DFT_N2 * SUBLANES
GATE_BLK = 512

BF16_ROWS = 16
CAST_ROWS = 128
POOL_HALF = max(POOL_WINDOWS) // 2
HALO = BF16_ROWS
FFN_TM = 1024
FFN_CUTS = (0, 256, 512, 768, 1024)
FFT1_SUB = 4
MIX_TM = 1024
MIX_SUB = 512

BF16 = jnp.bfloat16
F32 = jnp.float32


def _dot(a, b):
    return jnp.dot(a, b, preferred_element_type=F32)


def _inv_rms(x):
    return lax.rsqrt(jnp.mean(x * x, axis=-1, keepdims=True) + EPS)


def _rmsnorm_bf16(x, g):
    return (x * _inv_rms(x) * g).astype(BF16)


def _resident(shape):
    nd = len(shape)
    return pl.BlockSpec(shape, lambda *_: (0,) * nd, pipeline_mode=pl.Buffered(1))


def _layer(l, shape, col=0):
    return pl.BlockSpec((None,) + shape, lambda *_: (l, 0, col), pipeline_mode=pl.Buffered(1))


def _ffn_kernel(*refs, final_norm, n_cast):
    x_ref, g_ref, w1_ref, w3_ref, w2_ref, gfin_ref = refs[:6]
    cast_in = refs[6:6 + n_cast]
    o_ref = refs[6 + n_cast]
    cast_out = refs[7 + n_cast:7 + 2 * n_cast]
    act_ref = refs[7 + 2 * n_cast]

    for src, dst in zip(cast_in, cast_out):
        dst[...] = src[...].astype(BF16)

    for r0, r1 in zip(FFN_CUTS[:-1], FFN_CUTS[1:]):
        rows = slice(r0, r1)
        x = x_ref[rows, :]
        hu = (x * g_ref[...]).astype(BF16)
        rs = jnp.broadcast_to(_inv_rms(x), (r1 - r0, MXU_COLS))
        for c in range(0, D_FF, MXU_COLS):
            a = _dot(hu, w1_ref[:, c:c + MXU_COLS]) * rs
            b = _dot(hu, w3_ref[:, c:c + MXU_COLS]) * rs
            act_ref[rows, c:c + MXU_COLS] = (jax.nn.silu(a) * b).astype(BF16)
        y = x + 0.5 * _dot(act_ref[rows, :], w2_ref[...])
        if final_norm:
            ms = jnp.mean(y * y, axis=-1, keepdims=True)
            y = y * lax.rsqrt(ms + EPS) * gfin_ref[...]
        o_ref[rows, :] = y


def _cast_chunks(rows, steps):
    if rows % steps == 0 and (rows // steps) % BF16_ROWS == 0:
        return rows // steps, steps
    assert rows % CAST_ROWS == 0 and rows // CAST_ROWS <= steps
    return CAST_ROWS, rows // CAST_ROWS


def _ffn(x2d, l, g, weights, gfin, final_norm, cast=()):
    t, d = x2d.shape
    tm = FFN_TM
    steps = t // tm
    w1, w3, w2, wl = weights
    cast_in_specs, cast_out_specs, cast_out_shapes = [], [], []
    for stack, cl in cast:
        _, rows, cols = stack.shape
        chunk, n = _cast_chunks(rows, steps)
        cast_in_specs.append(pl.BlockSpec(
            (None, chunk, cols), lambda i, cl=cl, n=n: (cl, jnp.minimum(i, n - 1), 0)))
        cast_out_specs.append(pl.BlockSpec(
            (None, chunk, cols), lambda i, n=n: (0, jnp.minimum(i, n - 1), 0)))
        cast_out_shapes.append(jax.ShapeDtypeStruct((1, rows, cols), BF16))
    n_cast = len(cast)
    outs = pl.pallas_call(
        functools.partial(_ffn_kernel, final_norm=final_norm, n_cast=n_cast),
        out_shape=[jax.ShapeDtypeStruct((t, d), F32)] + cast_out_shapes,
        grid=(steps,),
        in_specs=[
            pl.BlockSpec((tm, d), lambda i: (i, 0)),
            _layer(l, (1, d)),
            _layer(wl, (d, D_FF)),
            _layer(wl, (d, D_FF)),
            _layer(wl, (D_FF, d)),
            _resident((1, d)),
        ] + cast_in_specs,
        out_specs=[pl.BlockSpec((tm, d), lambda i: (i, 0))] + cast_out_specs,
        scratch_shapes=[pltpu.VMEM((tm, D_FF), BF16)],
        compiler_params=pltpu.CompilerParams(
            dimension_semantics=("arbitrary",), vmem_limit_bytes=VMEM_LIMIT),
        name="ffn_final" if final_norm else "ffn",
    )(x2d, g, w1, w3, w2, gfin, *[stack for stack, _ in cast])
    return outs[0], outs[1:]


def _mix_in_kernel(x_ref, xp_ref, xn_ref, g_ref, win_ref, wconv_ref, wco_ref,
                   wpool_ref, pscale_ref, pf_ref, rest_ref, ext_ref, h_ref):
    tm = MIX_TM
    sub = MIX_SUB
    i = pl.program_id(1)
    nt = pl.num_programs(1)
    g = g_ref[...]

    last = tm - sub
    for r0 in range(0, tm, sub):
        rows = slice(r0, r0 + sub)
        h = _rmsnorm_bf16(x_ref[0, rows, :], g)
        h_ref[rows, :] = h
        pf_ref[0, rows, :] = _dot(h, win_ref[:, OFF_F:OFF_B]).astype(BF16)
        e0, e1 = HALO + r0, HALO + r0 + sub
        if r0 == 0:
            h = jnp.concatenate([_rmsnorm_bf16(xp_ref[0], g), h], axis=0)
            e0 -= HALO
        if r0 == last:
            h = jnp.concatenate([h, _rmsnorm_bf16(xn_ref[0], g)], axis=0)
            e1 += HALO
        n = e1 - e0
        row = lax.broadcasted_iota(jnp.int32, (n, 1), 0)
        inside = None
        if r0 == 0:
            inside = (row >= HALO) | (i > 0)
        if r0 == last:
            ok = (row < n - HALO) | (i < nt - 1)
            inside = ok if inside is None else inside & ok
        cv = _dot(h, win_ref[:, OFF_C:OFF_V]) * _dot(h, win_ref[:, OFF_V:OFF_P])
        pp = _dot(h, win_ref[:, OFF_P:OFF_G])
        if inside is not None:
            cv = jnp.where(inside, cv, 0.0)
            pp = jnp.where(inside, pp, 0.0)
        ext_ref[e0:e1, 0:CONV_WIDTH] = cv
        ext_ref[e0:e1, CONV_WIDTH:] = pp

    wconv = wconv_ref[...]
    pscale = pscale_ref[...]
    for r0 in range(0, tm, sub):
        rows = slice(r0, r0 + sub)
        e0 = HALO + r0
        h = h_ref[rows, :]
        conv = (wconv[0:1] * ext_ref[e0 - 1:e0 - 1 + sub, 0:CONV_WIDTH]
                + wconv[1:2] * ext_ref[e0:e0 + sub, 0:CONV_WIDTH]
                + wconv[2:3] * ext_ref[e0 + 1:e0 + 1 + sub, 0:CONV_WIDTH])
        bconv = (_dot(h, win_ref[:, OFF_B:OFF_C]) * conv).astype(BF16)

        t = (i * tm + r0 + lax.broadcasted_iota(jnp.int32, (sub, 1), 0)).astype(F32)
        for q in range(N_GROUPS):
            half = POOL_WINDOWS[q] // 2
            lo = CONV_WIDTH + q * POOL_GROUP
            win = ext_ref[e0 - half:e0 - half + sub, lo:lo + POOL_GROUP]
            for j in range(1 - half, half):
                win = win + ext_ref[e0 + j:e0 + j + sub, lo:lo + POOL_GROUP]
            count = jnp.minimum(t + half, float(SEQ)) - jnp.maximum(t - half, 0.0)
            pooled = (win / count - ext_ref[e0:e0 + sub, lo:lo + POOL_GROUP]).astype(BF16)

            cols = slice(q * OUT_GROUP, (q + 1) * OUT_GROUP)
            gcol = OFF_G + D_MODEL + q * OUT_GROUP
            gate_c = jax.nn.sigmoid(_dot(h, win_ref[:, gcol:gcol + OUT_GROUP]))
            gate_p = jax.nn.sigmoid(_dot(h, win_ref[:, gcol + D_MODEL:gcol + D_MODEL + OUT_GROUP]))
            y_c = _dot(bconv, wco_ref[:, cols])
            y_p = _dot(pooled, wpool_ref[q]) * pscale[:, cols]
            rest_ref[0, rows, cols] = gate_c * y_c + gate_p * y_p


def _mix_in(x, l, g, w_in, wl, w_conv, w_co, w_pool, pscale):
    b, s, d = x.shape
    tm = MIX_TM
    assert tm // MIX_SUB >= 2 and HALO >= POOL_HALF
    nt = s // tm
    hb = tm // HALO
    last_hb = s // HALO - 1
    return pl.pallas_call(
        _mix_in_kernel,
        out_shape=(
            jax.ShapeDtypeStruct((b, s, FOURIER_WIDTH), BF16),
            jax.ShapeDtypeStruct((b, s, d), F32),
        ),
        grid=(b, nt),
        in_specs=[
            pl.BlockSpec((1, tm, d), lambda bi, i: (bi, i, 0)),
            pl.BlockSpec((1, HALO, d), lambda bi, i: (bi, jnp.maximum(i * hb - 1, 0), 0)),
            pl.BlockSpec((1, HALO, d), lambda bi, i: (bi, jnp.minimum((i + 1) * hb, last_hb), 0)),
            _layer(l, (1, d)),
            _layer(wl, (d, IN_WIDTH)),
            _layer(l, (3, CONV_WIDTH)),
            _layer(l, (CONV_WIDTH, d)),
            pl.BlockSpec((None, N_GROUPS, POOL_GROUP, OUT_GROUP), lambda *_: (l, 0, 0, 0),
                         pipeline_mode=pl.Buffered(1)),
            _layer(l, (1, d)),
        ],
        out_specs=(
            pl.BlockSpec((1, tm, FOURIER_WIDTH), lambda bi, i: (bi, i, 0)),
            pl.BlockSpec((1, tm, d), lambda bi, i: (bi, i, 0)),
        ),
        scratch_shapes=[pltpu.VMEM((tm + 2 * HALO, CONV_WIDTH + POOL_WIDTH), F32),
                        pltpu.VMEM((tm, d), BF16)],
        compiler_params=pltpu.CompilerParams(
            dimension_semantics=("arbitrary", "arbitrary"), vmem_limit_bytes=VMEM_LIMIT),
        name="mix_in",
    )(x, x, x, g, w_in, w_conv, w_co, w_pool, pscale)


def _fft1_kernel(w_ref, wcc_ref, p_ref, o_ref):
    rows = FFT1_ROWS
    p = p_ref[0].astype(F32)
    a = []
    for s in range(FFT1_SUB):
        pf = p[:, s * SUBLANES:(s + 1) * SUBLANES, :].reshape(rows, FOURIER_WIDTH).astype(BF16)
        a.append(_dot(w_ref[...], pf).astype(BF16))
    tile = (DFT_N1, SUBLANES, FOURIER_GROUP)
    for q in range(N_GROUPS):
        cols = slice(q * FOURIER_GROUP, (q + 1) * FOURIER_GROUP)
        y = [_dot(jnp.concatenate([a_s[:rows, cols], a_s[rows:, cols]], axis=1), wcc_ref[...])
             for a_s in a]
        o_ref[0, 0, :, :, cols] = jnp.concatenate(
            [y_s[:, :FOURIER_GROUP].reshape(tile) for y_s in y], axis=1).astype(BF16)
        o_ref[0, 1, :, :, cols] = jnp.concatenate(
            [y_s[:, FOURIER_GROUP:].reshape(tile) for y_s in y], axis=1).astype(BF16)


def _fft1(p4, w1k, wcc):
    b = p4.shape[0]
    n2_blk = FFT1_SUB * SUBLANES
    return pl.pallas_call(
        _fft1_kernel,
        out_shape=jax.ShapeDtypeStruct((b, 2, DFT_N1, DFT_N2, FOURIER_WIDTH), BF16),
        grid=(b, DFT_N2 // n2_blk),
        in_specs=[
            _resident(w1k.shape),
            _resident(wcc.shape),
            pl.BlockSpec((1, DFT_N1, n2_blk, FOURIER_WIDTH), lambda bi, j: (bi, 0, j, 0)),
        ],
        out_specs=pl.BlockSpec((1, 2, DFT_N1, n2_blk, FOURIER_WIDTH),
                               lambda bi, j: (bi, 0, 0, j, 0)),
        compiler_params=pltpu.CompilerParams(
            dimension_semantics=("arbitrary", "arbitrary"), vmem_limit_bytes=VMEM_LIMIT),
        name="fft1",
    )(w1k, wcc, p4)


def _mix_out_kernel(x_ref, rest_ref, y_ref, gd_ref, g_ref, wga_ref, wgb_ref, wf_ref, wo_ref,
                    o_ref, m_ref, f_ref):
    t = OUT_TOK
    for j in range(SUBLANES):
        yj = jnp.concatenate([y_ref[0, 0, j], y_ref[0, 1, j]], axis=0)
        fj = _dot(gd_ref[0, j], yj)
        for c in range(FOURIER_WIDTH // LANES):
            f_ref[c, pl.ds(j, DFT_N2, stride=SUBLANES), :] = fj[:, c * LANES:(c + 1) * LANES]

    x = x_ref[0].reshape(t, D_MODEL)
    hu = (x * g_ref[...]).astype(BF16)
    rs = jnp.broadcast_to(_inv_rms(x), (t, OUT_GROUP))
    rest = rest_ref[0].reshape(t, D_MODEL)
    for q in range(N_GROUPS):
        cols = slice(q * OUT_GROUP, (q + 1) * OUT_GROUP)
        wg_ref = wga_ref if q < GATE_BLK // OUT_GROUP else wgb_ref
        gc = (q * OUT_GROUP) % GATE_BLK
        gate_f = jax.nn.sigmoid(_dot(hu, wg_ref[:, gc:gc + OUT_GROUP]) * rs)
        y_f = _dot(f_ref[q].astype(BF16), wf_ref[q])
        m_ref[:, cols] = (gate_f * y_f + rest[:, cols]).astype(BF16)
    o = x + _dot(m_ref[...], wo_ref[...])
    o_ref[0] = o.reshape(DFT_N2, SUBLANES, D_MODEL)


def _mix_out(x4, rest4, y5, l, gd, g, w_in, wl, wf, wo):
    b = x4.shape[0]
    d = D_MODEL
    tok_blk = (1, DFT_N2, SUBLANES, d)
    tok = lambda ki, bi: (bi, 0, ki, 0)
    gate_col = OFF_G // GATE_BLK
    return pl.pallas_call(
        _mix_out_kernel,
        out_shape=jax.ShapeDtypeStruct(x4.shape, F32),
        grid=(DFT_N1 // SUBLANES, b),
        in_specs=[
            pl.BlockSpec(tok_blk, tok),
            pl.BlockSpec(tok_blk, tok),
            pl.BlockSpec((1, 2, SUBLANES, DFT_N2, FOURIER_WIDTH), lambda ki, bi: (bi, 0, ki, 0, 0)),
            pl.BlockSpec((1, SUBLANES, DFT_N2, 2 * DFT_N2), lambda ki, bi: (ki, 0, 0, 0)),
            _layer(l, (1, d)),
            _layer(wl, (d, GATE_BLK), gate_col),
            _layer(wl, (d, GATE_BLK), gate_col + 1),
            pl.BlockSpec((None, N_GROUPS, FOURIER_GROUP, OUT_GROUP), lambda *_: (l, 0, 0, 0),
                         pipeline_mode=pl.Buffered(1)),
            _layer(l, (d, d)),
        ],
        out_specs=pl.BlockSpec(tok_blk, tok),
        scratch_shapes=[pltpu.VMEM((OUT_TOK, d), BF16),
                        pltpu.VMEM((FOURIER_WIDTH // LANES, OUT_TOK, LANES), F32)],
        compiler_params=pltpu.CompilerParams(
            dimension_semantics=("arbitrary", "arbitrary"), vmem_limit_bytes=VMEM_LIMIT),
        name="mix_out",
    )(x4, rest4, y5, gd, g, w_in, w_in, wf, wo)


def _dft_tables():
    two_pi = 2.0 * np.pi
    r8 = SUBLANES
    c = np.arange(FOURIER_GROUP)
    ang = two_pi * ((c[:, None] * c[None, :]) % FOURIER_GROUP) / FOURIER_GROUP
    cs, sn = np.cos(ang), np.sin(ang)
    wcc = np.block([[cs, -sn], [sn, cs]])

    k1 = np.arange(DFT_N1)
    ang = two_pi * ((k1[:, None] * k1[None, :]) % DFT_N1) / DFT_N1
    w1 = np.stack([np.cos(ang), -np.sin(ang)], axis=0)
    eye = np.eye(r8)
    w1k = (w1[:, :, None, :, None] * eye[None, None, :, None, :]).reshape(2 * FFT1_ROWS, FFT1_ROWS)

    nkb = DFT_N1 // r8
    kk = (r8 * np.arange(nkb)[:, None, None] + np.arange(r8)[None, :, None]
          + DFT_N1 * np.arange(DFT_N2)[None, None, :])
    ang = two_pi * ((kk[..., None] * np.arange(DFT_N2)) % SEQ) / SEQ
    norm = 1.0 / math.sqrt(FOURIER_GROUP * SEQ)
    gd = (np.stack([np.cos(ang), np.sin(ang)], axis=3) * norm).reshape(nkb, r8, DFT_N2, 2 * DFT_N2)
    as_bf16 = lambda a: jnp.asarray(a.astype(BF16))
    return as_bf16(wcc), as_bf16(w1k), as_bf16(gd)


def kernel(x, g_ffn1, w1_a, w3_a, w2_a, g_mix, w_in, w_fourier, w_conv, w_conv_out,
           w_pool, pool_scale, w_o, g_ffn2, w1_b, w3_b, w2_b, g_final):
    b, s, d = x.shape
    assert (s, d) == (SEQ, D_MODEL)
    wcc, w1k, gd = _dft_tables()
    row = lambda v: v.reshape(DEPTH, 1, d)
    gfin = g_final.reshape(1, d)
    g_ffn1, g_mix, g_ffn2, pool_scale = row(g_ffn1), row(g_mix), row(g_ffn2), row(pool_scale)
    w_o, w_fourier = w_o.astype(BF16), w_fourier.astype(BF16)
    w_pool, w_conv_out = w_pool.astype(BF16), w_conv_out.astype(BF16)
    view = (b, DFT_N2, DFT_N1, d)
    ffn_a = (w1_a[:1].astype(BF16), w3_a[:1].astype(BF16), w2_a[:1].astype(BF16), 0)
    x = x.reshape(b * s, d)
    for l in range(DEPTH):
        x, (w_in_l, w1, w3, w2) = _ffn(x, l, g_ffn1, ffn_a, gfin, False,
                                       cast=((w_in, l), (w1_b, l), (w3_b, l), (w2_b, l)))
        ffn_b = (w1, w3, w2, 0)
        x = x.reshape(b, s, d)
        pf, rest = _mix_in(x, l, g_mix, w_in_l, 0, w_conv, w_conv_out, w_pool, pool_scale)
        y = _fft1(pf.reshape(b, DFT_N1, DFT_N2, FOURIER_WIDTH), w1k, wcc)
        x = _mix_out(x.reshape(view), rest.reshape(view), y, l, gd, g_mix, w_in_l, 0, w_fourier,
                     w_o).reshape(b * s, d)
        nxt = l + 1
        cast = ((w1_a, nxt), (w3_a, nxt), (w2_a, nxt)) if nxt < DEPTH else ()
        x, cast_out = _ffn(x, l, g_ffn2, ffn_b, gfin, nxt == DEPTH, cast=cast)
        if cast_out:
            ffn_a = tuple(cast_out) + (0,)
    return x.reshape(b, s, d)
```

```python
import functools
import math

import numpy as np
import jax
import jax.numpy as jnp
from jax import lax
from jax.experimental import pallas as pl
from jax.experimental.pallas import tpu as pltpu

D_MODEL = 1024
SEQ = 8192
DEPTH = 4
N_GROUPS = 4
FOURIER_WIDTH = 512
FOURIER_GROUP = FOURIER_WIDTH // N_GROUPS
CONV_WIDTH = 512
POOL_WIDTH = 512
POOL_GROUP = POOL_WIDTH // N_GROUPS
POOL_WINDOWS = (2, 4, 8, 16)
OUT_GROUP = D_MODEL // N_GROUPS
D_FF = 2816
EPS = 1e-6

OFF_F = 0
OFF_B = OFF_F + FOURIER_WIDTH
OFF_C = OFF_B + CONV_WIDTH
OFF_V = OFF_C + CONV_WIDTH
OFF_P = OFF_V + CONV_WIDTH
OFF_G = OFF_P + POOL_WIDTH
IN_WIDTH = OFF_G + 3 * D_MODEL

SUBLANES = 8
LANES = 128
MXU_COLS = 256
VMEM_LIMIT = 56 * 1024 * 1024

DFT_N1 = 64
DFT_N2 = SEQ // DFT_N1
FFT1_ROWS = DFT_N1 * SUBLANES
OUT_TOK = DFT_N2 * SUBLANES
GATE_BLK = 512

BF16_ROWS = 16
CAST_ROWS = 128
POOL_HALF = max(POOL_WINDOWS) // 2
HALO = BF16_ROWS
FFN_TM = 1024
FFN_SUB = 256
FFT1_SUB = 4
MIX_TM = 1024
MIX_SUB = 512

BF16 = jnp.bfloat16
F32 = jnp.float32


def _dot(a, b):
    return jnp.dot(a, b, preferred_element_type=F32)


def _rmsnorm_bf16(x, g):
    ms = jnp.mean(x * x, axis=-1, keepdims=True)
    return (x * lax.rsqrt(ms + EPS) * g).astype(BF16)


def _resident(shape):
    nd = len(shape)
    return pl.BlockSpec(shape, lambda *_: (0,) * nd, pipeline_mode=pl.Buffered(1))


def _layer(l, shape, col=0):
    return pl.BlockSpec((None,) + shape, lambda *_: (l, 0, col), pipeline_mode=pl.Buffered(1))


def _ffn_kernel(*refs, final_norm, n_cast):
    x_ref, g_ref, w1_ref, w3_ref, w2_ref, gfin_ref = refs[:6]
    cast_in = refs[6:6 + n_cast]
    o_ref = refs[6 + n_cast]
    cast_out = refs[7 + n_cast:7 + 2 * n_cast]
    act_ref = refs[7 + 2 * n_cast]

    for src, dst in zip(cast_in, cast_out):
        dst[...] = src[...].astype(BF16)

    for r0 in range(0, FFN_TM, FFN_SUB):
        rows = slice(r0, r0 + FFN_SUB)
        x = x_ref[rows, :]
        h = _rmsnorm_bf16(x, g_ref[...])
        for c in range(0, D_FF, MXU_COLS):
            a = _dot(h, w1_ref[:, c:c + MXU_COLS])
            b = _dot(h, w3_ref[:, c:c + MXU_COLS])
            act_ref[rows, c:c + MXU_COLS] = (jax.nn.silu(a) * b).astype(BF16)
        y = x + 0.5 * _dot(act_ref[rows, :], w2_ref[...])
        if final_norm:
            ms = jnp.mean(y * y, axis=-1, keepdims=True)
            y = y * lax.rsqrt(ms + EPS) * gfin_ref[...]
        o_ref[rows, :] = y


def _cast_chunks(rows, steps):
    if rows % steps == 0 and (rows // steps) % BF16_ROWS == 0:
        return rows // steps, steps
    assert rows % CAST_ROWS == 0 and rows // CAST_ROWS <= steps
    return CAST_ROWS, rows // CAST_ROWS


def _ffn(x2d, l, g, weights, gfin, final_norm, cast=()):
    t, d = x2d.shape
    tm = FFN_TM
    steps = t // tm
    w1, w3, w2, wl = weights
    cast_in_specs, cast_out_specs, cast_out_shapes = [], [], []
    for stack, cl in cast:
        _, rows, cols = stack.shape
        chunk, n = _cast_chunks(rows, steps)
        cast_in_specs.append(pl.BlockSpec(
            (None, chunk, cols), lambda i, cl=cl, n=n: (cl, jnp.minimum(i, n - 1), 0)))
        cast_out_specs.append(pl.BlockSpec(
            (None, chunk, cols), lambda i, n=n: (0, jnp.minimum(i, n - 1), 0)))
        cast_out_shapes.append(jax.ShapeDtypeStruct((1, rows, cols), BF16))
    n_cast = len(cast)
    outs = pl.pallas_call(
        functools.partial(_ffn_kernel, final_norm=final_norm, n_cast=n_cast),
        out_shape=[jax.ShapeDtypeStruct((t, d), F32)] + cast_out_shapes,
        grid=(steps,),
        in_specs=[
            pl.BlockSpec((tm, d), lambda i: (i, 0)),
            _layer(l, (1, d)),
            _layer(wl, (d, D_FF)),
            _layer(wl, (d, D_FF)),
            _layer(wl, (D_FF, d)),
            _resident((1, d)),
        ] + cast_in_specs,
        out_specs=[pl.BlockSpec((tm, d), lambda i: (i, 0))] + cast_out_specs,
        scratch_shapes=[pltpu.VMEM((tm, D_FF), BF16)],
        compiler_params=pltpu.CompilerParams(
            dimension_semantics=("arbitrary",), vmem_limit_bytes=VMEM_LIMIT),
        name="ffn_final" if final_norm else "ffn",
    )(x2d, g, w1, w3, w2, gfin, *[stack for stack, _ in cast])
    return outs[0], outs[1:]


def _mix_in_kernel(x_ref, xp_ref, xn_ref, g_ref, win_ref, wconv_ref, wco_ref,
                   wpool_ref, pscale_ref, pf_ref, rest_ref, ext_ref, h_ref):
    tm = MIX_TM
    sub = MIX_SUB
    i = pl.program_id(1)
    nt = pl.num_programs(1)
    g = g_ref[...]

    last = tm - sub
    for r0 in range(0, tm, sub):
        rows = slice(r0, r0 + sub)
        h = _rmsnorm_bf16(x_ref[0, rows, :], g)
        h_ref[rows, :] = h
        pf_ref[0, rows, :] = _dot(h, win_ref[:, OFF_F:OFF_B]).astype(BF16)
        e0, e1 = HALO + r0, HALO + r0 + sub
        if r0 == 0:
            h = jnp.concatenate([_rmsnorm_bf16(xp_ref[0], g), h], axis=0)
            e0 -= HALO
        if r0 == last:
            h = jnp.concatenate([h, _rmsnorm_bf16(xn_ref[0], g)], axis=0)
            e1 += HALO
        n = e1 - e0
        row = lax.broadcasted_iota(jnp.int32, (n, 1), 0)
        inside = None
        if r0 == 0:
            inside = (row >= HALO) | (i > 0)
        if r0 == last:
            ok = (row < n - HALO) | (i < nt - 1)
            inside = ok if inside is None else inside & ok
        cv = _dot(h, win_ref[:, OFF_C:OFF_V]) * _dot(h, win_ref[:, OFF_V:OFF_P])
        pp = _dot(h, win_ref[:, OFF_P:OFF_G])
        if inside is not None:
            cv = jnp.where(inside, cv, 0.0)
            pp = jnp.where(inside, pp, 0.0)
        ext_ref[e0:e1, 0:CONV_WIDTH] = cv
        ext_ref[e0:e1, CONV_WIDTH:] = pp

    wconv = wconv_ref[...]
    pscale = pscale_ref[...]
    for r0 in range(0, tm, sub):
        rows = slice(r0, r0 + sub)
        e0 = HALO + r0
        h = h_ref[rows, :]
        conv = (wconv[0:1] * ext_ref[e0 - 1:e0 - 1 + sub, 0:CONV_WIDTH]
                + wconv[1:2] * ext_ref[e0:e0 + sub, 0:CONV_WIDTH]
                + wconv[2:3] * ext_ref[e0 + 1:e0 + 1 + sub, 0:CONV_WIDTH])
        bconv = (_dot(h, win_ref[:, OFF_B:OFF_C]) * conv).astype(BF16)

        t = (i * tm + r0 + lax.broadcasted_iota(jnp.int32, (sub, 1), 0)).astype(F32)
        for q in range(N_GROUPS):
            half = POOL_WINDOWS[q] // 2
            lo = CONV_WIDTH + q * POOL_GROUP
            win = ext_ref[e0 - half:e0 - half + sub, lo:lo + POOL_GROUP]
            for j in range(1 - half, half):
                win = win + ext_ref[e0 + j:e0 + j + sub, lo:lo + POOL_GROUP]
            count = jnp.minimum(t + half, float(SEQ)) - jnp.maximum(t - half, 0.0)
            pooled = (win / count - ext_ref[e0:e0 + sub, lo:lo + POOL_GROUP]).astype(BF16)

            cols = slice(q * OUT_GROUP, (q + 1) * OUT_GROUP)
            gcol = OFF_G + D_MODEL + q * OUT_GROUP
            gate_c = jax.nn.sigmoid(_dot(h, win_ref[:, gcol:gcol + OUT_GROUP]))
            gate_p = jax.nn.sigmoid(_dot(h, win_ref[:, gcol + D_MODEL:gcol + D_MODEL + OUT_GROUP]))
            y_c = _dot(bconv, wco_ref[:, cols])
            y_p = _dot(pooled, wpool_ref[q]) * pscale[:, cols]
            rest_ref[0, rows, cols] = gate_c * y_c + gate_p * y_p


def _mix_in(x, l, g, w_in, wl, w_conv, w_co, w_pool, pscale):
    b, s, d = x.shape
    tm = MIX_TM
    assert tm // MIX_SUB >= 2 and HALO >= POOL_HALF
    nt = s // tm
    hb = tm // HALO
    last_hb = s // HALO - 1
    return pl.pallas_call(
        _mix_in_kernel,
        out_shape=(
            jax.ShapeDtypeStruct((b, s, FOURIER_WIDTH), BF16),
            jax.ShapeDtypeStruct((b, s, d), F32),
        ),
        grid=(b, nt),
        in_specs=[
            pl.BlockSpec((1, tm, d), lambda bi, i: (bi, i, 0)),
            pl.BlockSpec((1, HALO, d), lambda bi, i: (bi, jnp.maximum(i * hb - 1, 0), 0)),
            pl.BlockSpec((1, HALO, d), lambda bi, i: (bi, jnp.minimum((i + 1) * hb, last_hb), 0)),
            _layer(l, (1, d)),
            _layer(wl, (d, IN_WIDTH)),
            _layer(l, (3, CONV_WIDTH)),
            _layer(l, (CONV_WIDTH, d)),
            pl.BlockSpec((None, N_GROUPS, POOL_GROUP, OUT_GROUP), lambda *_: (l, 0, 0, 0),
                         pipeline_mode=pl.Buffered(1)),
            _layer(l, (1, d)),
        ],
        out_specs=(
            pl.BlockSpec((1, tm, FOURIER_WIDTH), lambda bi, i: (bi, i, 0)),
            pl.BlockSpec((1, tm, d), lambda bi, i: (bi, i, 0)),
        ),
        scratch_shapes=[pltpu.VMEM((tm + 2 * HALO, CONV_WIDTH + POOL_WIDTH), F32),
                        pltpu.VMEM((tm, d), BF16)],
        compiler_params=pltpu.CompilerParams(
            dimension_semantics=("arbitrary", "arbitrary"), vmem_limit_bytes=VMEM_LIMIT),
        name="mix_in",
    )(x, x, x, g, w_in, w_conv, w_co, w_pool, pscale)


def _fft1_kernel(w_ref, wcc_ref, p_ref, o_ref):
    rows = FFT1_ROWS
    p = p_ref[0].astype(F32)
    a = []
    for s in range(FFT1_SUB):
        pf = p[:, s * SUBLANES:(s + 1) * SUBLANES, :].reshape(rows, FOURIER_WIDTH).astype(BF16)
        a.append(_dot(w_ref[...], pf).astype(BF16))
    tile = (DFT_N1, SUBLANES, FOURIER_GROUP)
    for q in range(N_GROUPS):
        cols = slice(q * FOURIER_GROUP, (q + 1) * FOURIER_GROUP)
        y = [_dot(jnp.concatenate([a_s[:rows, cols], a_s[rows:, cols]], axis=1), wcc_ref[...])
             for a_s in a]
        o_ref[0, 0, :, :, cols] = jnp.concatenate(
            [y_s[:, :FOURIER_GROUP].reshape(tile) for y_s in y], axis=1).astype(BF16)
        o_ref[0, 1, :, :, cols] = jnp.concatenate(
            [y_s[:, FOURIER_GROUP:].reshape(tile) for y_s in y], axis=1).astype(BF16)


def _fft1(p4, w1k, wcc):
    b = p4.shape[0]
    n2_blk = FFT1_SUB * SUBLANES
    return pl.pallas_call(
        _fft1_kernel,
        out_shape=jax.ShapeDtypeStruct((b, 2, DFT_N1, DFT_N2, FOURIER_WIDTH), BF16),
        grid=(b, DFT_N2 // n2_blk),
        in_specs=[
            _resident(w1k.shape),
            _resident(wcc.shape),
            pl.BlockSpec((1, DFT_N1, n2_blk, FOURIER_WIDTH), lambda bi, j: (bi, 0, j, 0)),
        ],
        out_specs=pl.BlockSpec((1, 2, DFT_N1, n2_blk, FOURIER_WIDTH),
                               lambda bi, j: (bi, 0, 0, j, 0)),
        compiler_params=pltpu.CompilerParams(
            dimension_semantics=("arbitrary", "arbitrary"), vmem_limit_bytes=VMEM_LIMIT),
        name="fft1",
    )(w1k, wcc, p4)


def _mix_out_kernel(x_ref, rest_ref, y_ref, gd_ref, g_ref, wga_ref, wgb_ref, wf_ref, wo_ref,
                    o_ref, m_ref, f_ref):
    t = OUT_TOK
    for j in range(SUBLANES):
        yj = jnp.concatenate([y_ref[0, 0, j], y_ref[0, 1, j]], axis=0)
        fj = _dot(gd_ref[0, j], yj)
        for c in range(FOURIER_WIDTH // LANES):
            f_ref[c, pl.ds(j, DFT_N2, stride=SUBLANES), :] = fj[:, c * LANES:(c + 1) * LANES]

    x = x_ref[0].reshape(t, D_MODEL)
    h = _rmsnorm_bf16(x, g_ref[...])
    rest = rest_ref[0].reshape(t, D_MODEL)
    for q in range(N_GROUPS):
        cols = slice(q * OUT_GROUP, (q + 1) * OUT_GROUP)
        wg_ref = wga_ref if q < GATE_BLK // OUT_GROUP else wgb_ref
        gc = (q * OUT_GROUP) % GATE_BLK
        gate_f = jax.nn.sigmoid(_dot(h, wg_ref[:, gc:gc + OUT_GROUP]))
        y_f = _dot(f_ref[q].astype(BF16), wf_ref[q])
        m_ref[:, cols] = (gate_f * y_f + rest[:, cols]).astype(BF16)
    o = x + _dot(m_ref[...], wo_ref[...])
    o_ref[0] = o.reshape(DFT_N2, SUBLANES, D_MODEL)


def _mix_out(x4, rest4, y5, l, gd, g, w_in, wl, wf, wo):
    b = x4.shape[0]
    d = D_MODEL
    tok_blk = (1, DFT_N2, SUBLANES, d)
    tok = lambda ki, bi: (bi, 0, ki, 0)
    gate_col = OFF_G // GATE_BLK
    return pl.pallas_call(
        _mix_out_kernel,
        out_shape=jax.ShapeDtypeStruct(x4.shape, F32),
        grid=(DFT_N1 // SUBLANES, b),
        in_specs=[
            pl.BlockSpec(tok_blk, tok),
            pl.BlockSpec(tok_blk, tok),
            pl.BlockSpec((1, 2, SUBLANES, DFT_N2, FOURIER_WIDTH), lambda ki, bi: (bi, 0, ki, 0, 0)),
            pl.BlockSpec((1, SUBLANES, DFT_N2, 2 * DFT_N2), lambda ki, bi: (ki, 0, 0, 0)),
            _layer(l, (1, d)),
            _layer(wl, (d, GATE_BLK), gate_col),
            _layer(wl, (d, GATE_BLK), gate_col + 1),
            pl.BlockSpec((None, N_GROUPS, FOURIER_GROUP, OUT_GROUP), lambda *_: (l, 0, 0, 0),
                         pipeline_mode=pl.Buffered(1)),
            _layer(l, (d, d)),
        ],
        out_specs=pl.BlockSpec(tok_blk, tok),
        scratch_shapes=[pltpu.VMEM((OUT_TOK, d), BF16),
                        pltpu.VMEM((FOURIER_WIDTH // LANES, OUT_TOK, LANES), F32)],
        compiler_params=pltpu.CompilerParams(
            dimension_semantics=("arbitrary", "arbitrary"), vmem_limit_bytes=VMEM_LIMIT),
        name="mix_out",
    )(x4, rest4, y5, gd, g, w_in, w_in, wf, wo)


def _dft_tables():
    two_pi = 2.0 * np.pi
    r8 = SUBLANES
    c = np.arange(FOURIER_GROUP)
    ang = two_pi * ((c[:, None] * c[None, :]) % FOURIER_GROUP) / FOURIER_GROUP
    cs, sn = np.cos(ang), np.sin(ang)
    wcc = np.block([[cs, -sn], [sn, cs]])

    k1 = np.arange(DFT_N1)
    ang = two_pi * ((k1[:, None] * k1[None, :]) % DFT_N1) / DFT_N1
    w1 = np.stack([np.cos(ang), -np.sin(ang)], axis=0)
    eye = np.eye(r8)
    w1k = (w1[:, :, None, :, None] * eye[None, None, :, None, :]).reshape(2 * FFT1_ROWS, FFT1_ROWS)

    nkb = DFT_N1 // r8
    kk = (r8 * np.arange(nkb)[:, None, None] + np.arange(r8)[None, :, None]
          + DFT_N1 * np.arange(DFT_N2)[None, None, :])
    ang = two_pi * ((kk[..., None] * np.arange(DFT_N2)) % SEQ) / SEQ
    norm = 1.0 / math.sqrt(FOURIER_GROUP * SEQ)
    gd = (np.stack([np.cos(ang), np.sin(ang)], axis=3) * norm).reshape(nkb, r8, DFT_N2, 2 * DFT_N2)
    as_bf16 = lambda a: jnp.asarray(a.astype(np.float32)).astype(BF16)
    return as_bf16(wcc), as_bf16(w1k), as_bf16(gd)


def kernel(x, g_ffn1, w1_a, w3_a, w2_a, g_mix, w_in, w_fourier, w_conv, w_conv_out,
           w_pool, pool_scale, w_o, g_ffn2, w1_b, w3_b, w2_b, g_final):
    b, s, d = x.shape
    assert (s, d) == (SEQ, D_MODEL)
    wcc, w1k, gd = _dft_tables()
    row = lambda v: v.reshape(DEPTH, 1, d)
    gfin = g_final.reshape(1, d)
    g_ffn1, g_mix, g_ffn2, pool_scale = row(g_ffn1), row(g_mix), row(g_ffn2), row(pool_scale)
    w_o, w_fourier = w_o.astype(BF16), w_fourier.astype(BF16)
    w_pool, w_conv_out = w_pool.astype(BF16), w_conv_out.astype(BF16)
    view = (b, DFT_N2, DFT_N1, d)
    ffn_a = (w1_a[:1].astype(BF16), w3_a[:1].astype(BF16), w2_a[:1].astype(BF16), 0)
    x = x.reshape(b * s, d)
    for l in range(DEPTH):
        x, (w_in_l, w1, w3, w2) = _ffn(x, l, g_ffn1, ffn_a, gfin, False,
                                       cast=((w_in, l), (w1_b, l), (w3_b, l), (w2_b, l)))
        ffn_b = (w1, w3, w2, 0)
        x = x.reshape(b, s, d)
        pf, rest = _mix_in(x, l, g_mix, w_in_l, 0, w_conv, w_conv_out, w_pool, pool_scale)
        y = _fft1(pf.reshape(b, DFT_N1, DFT_N2, FOURIER_WIDTH), w1k, wcc)
        x = _mix_out(x.reshape(view), rest.reshape(view), y, l, gd, g_mix, w_in_l, 0, w_fourier,
                     w_o).reshape(b * s, d)
        nxt = l + 1
        cast = ((w1_a, nxt), (w3_a, nxt), (w2_a, nxt)) if nxt < DEPTH else ()
        x, cast_out = _ffn(x, l, g_ffn2, ffn_b, gfin, nxt == DEPTH, cast=cast)
        if cast_out:
            ffn_a = tuple(cast_out) + (0,)
    return x.reshape(b, s, d)
```

```python
import functools
import math

import numpy as np
import jax
import jax.numpy as jnp
from jax import lax
from jax.experimental import pallas as pl
from jax.experimental.pallas import tpu as pltpu

D_MODEL = 1024
SEQ = 8192
DEPTH = 4
N_GROUPS = 4
FOURIER_WIDTH = 512
FOURIER_GROUP = FOURIER_WIDTH // N_GROUPS
CONV_WIDTH = 512
POOL_WIDTH = 512
POOL_GROUP = POOL_WIDTH // N_GROUPS
POOL_WINDOWS = (2, 4, 8, 16)
OUT_GROUP = D_MODEL // N_GROUPS
D_FF = 2816
EPS = 1e-6

OFF_F = 0
OFF_B = OFF_F + FOURIER_WIDTH
OFF_C = OFF_B + CONV_WIDTH
OFF_V = OFF_C + CONV_WIDTH
OFF_P = OFF_V + CONV_WIDTH
OFF_G = OFF_P + POOL_WIDTH
IN_WIDTH = OFF_G + 3 * D_MODEL

SUBLANES = 8
LANES = 128
MXU_COLS = 256
VMEM_LIMIT = 56 * 1024 * 1024

DFT_N1 = 64
DFT_N2 = SEQ // DFT_N1
FFT1_ROWS = DFT_N1 * SUBLANES
OUT_TOK = DFT_N2 * SUBLANES
GATE_BLK = 512

BF16_ROWS = 16
CAST_ROWS = 128
POOL_HALF = max(POOL_WINDOWS) // 2
HALO = BF16_ROWS
FFN_TM = 1024
FFN_SUB = 256
FFT1_SUB = 8
MIX_TM = 1024
MIX_SUB = 512

BF16 = jnp.bfloat16
F32 = jnp.float32


def _dot(a, b):
    return jnp.dot(a, b, preferred_element_type=F32)


def _rmsnorm_bf16(x, g):
    ms = jnp.mean(x * x, axis=-1, keepdims=True)
    return (x * lax.rsqrt(ms + EPS) * g).astype(BF16)


def _resident(shape):
    nd = len(shape)
    return pl.BlockSpec(shape, lambda *_: (0,) * nd, pipeline_mode=pl.Buffered(1))


def _layer(l, shape, col=0):
    return pl.BlockSpec((None,) + shape, lambda *_: (l, 0, col), pipeline_mode=pl.Buffered(1))


def _ffn_kernel(*refs, final_norm, n_cast):
    x_ref, g_ref, w1_ref, w3_ref, w2_ref, gfin_ref = refs[:6]
    cast_in = refs[6:6 + n_cast]
    o_ref = refs[6 + n_cast]
    cast_out = refs[7 + n_cast:7 + 2 * n_cast]
    act_ref = refs[7 + 2 * n_cast]

    for src, dst in zip(cast_in, cast_out):
        dst[...] = src[...].astype(BF16)

    for r0 in range(0, FFN_TM, FFN_SUB):
        rows = slice(r0, r0 + FFN_SUB)
        x = x_ref[rows, :]
        h = _rmsnorm_bf16(x, g_ref[...])
        for c in range(0, D_FF, MXU_COLS):
            a = _dot(h, w1_ref[:, c:c + MXU_COLS])
            b = _dot(h, w3_ref[:, c:c + MXU_COLS])
            act_ref[rows, c:c + MXU_COLS] = (jax.nn.silu(a) * b).astype(BF16)
        y = x + 0.5 * _dot(act_ref[rows, :], w2_ref[...])
        if final_norm:
            ms = jnp.mean(y * y, axis=-1, keepdims=True)
            y = y * lax.rsqrt(ms + EPS) * gfin_ref[...]
        o_ref[rows, :] = y


def _cast_chunks(rows, steps):
    if rows % steps == 0 and (rows // steps) % BF16_ROWS == 0:
        return rows // steps, steps
    assert rows % CAST_ROWS == 0 and rows // CAST_ROWS <= steps
    return CAST_ROWS, rows // CAST_ROWS


def _ffn(x2d, l, g, weights, gfin, final_norm, cast=()):
    t, d = x2d.shape
    tm = FFN_TM
    steps = t // tm
    w1, w3, w2, wl = weights
    cast_in_specs, cast_out_specs, cast_out_shapes = [], [], []
    for stack, cl in cast:
        _, rows, cols = stack.shape
        chunk, n = _cast_chunks(rows, steps)
        cast_in_specs.append(pl.BlockSpec(
            (None, chunk, cols), lambda i, cl=cl, n=n: (cl, jnp.minimum(i, n - 1), 0)))
        cast_out_specs.append(pl.BlockSpec(
            (None, chunk, cols), lambda i, n=n: (0, jnp.minimum(i, n - 1), 0)))
        cast_out_shapes.append(jax.ShapeDtypeStruct((1, rows, cols), BF16))
    n_cast = len(cast)
    outs = pl.pallas_call(
        functools.partial(_ffn_kernel, final_norm=final_norm, n_cast=n_cast),
        out_shape=[jax.ShapeDtypeStruct((t, d), F32)] + cast_out_shapes,
        grid=(steps,),
        in_specs=[
            pl.BlockSpec((tm, d), lambda i: (i, 0)),
            _layer(l, (1, d)),
            _layer(wl, (d, D_FF)),
            _layer(wl, (d, D_FF)),
            _layer(wl, (D_FF, d)),
            _resident((1, d)),
        ] + cast_in_specs,
        out_specs=[pl.BlockSpec((tm, d), lambda i: (i, 0))] + cast_out_specs,
        scratch_shapes=[pltpu.VMEM((tm, D_FF), BF16)],
        compiler_params=pltpu.CompilerParams(
            dimension_semantics=("arbitrary",), vmem_limit_bytes=VMEM_LIMIT),
        name="ffn_final" if final_norm else "ffn",
    )(x2d, g, w1, w3, w2, gfin, *[stack for stack, _ in cast])
    return outs[0], outs[1:]


def _mix_in_kernel(x_ref, xp_ref, xn_ref, g_ref, win_ref, wconv_ref, wco_ref,
                   wpool_ref, pscale_ref, pf_ref, rest_ref, ext_ref, h_ref):
    tm = MIX_TM
    sub = MIX_SUB
    i = pl.program_id(1)
    nt = pl.num_programs(1)
    g = g_ref[...]

    last = tm - sub
    for r0 in range(0, tm, sub):
        rows = slice(r0, r0 + sub)
        h = _rmsnorm_bf16(x_ref[0, rows, :], g)
        h_ref[rows, :] = h
        pf_ref[0, rows, :] = _dot(h, win_ref[:, OFF_F:OFF_B]).astype(BF16)
        e0, e1 = HALO + r0, HALO + r0 + sub
        if r0 == 0:
            h = jnp.concatenate([_rmsnorm_bf16(xp_ref[0], g), h], axis=0)
            e0 -= HALO
        if r0 == last:
            h = jnp.concatenate([h, _rmsnorm_bf16(xn_ref[0], g)], axis=0)
            e1 += HALO
        n = e1 - e0
        row = lax.broadcasted_iota(jnp.int32, (n, 1), 0)
        inside = None
        if r0 == 0:
            inside = (row >= HALO) | (i > 0)
        if r0 == last:
            ok = (row < n - HALO) | (i < nt - 1)
            inside = ok if inside is None else inside & ok
        cv = _dot(h, win_ref[:, OFF_C:OFF_V]) * _dot(h, win_ref[:, OFF_V:OFF_P])
        pp = _dot(h, win_ref[:, OFF_P:OFF_G])
        if inside is not None:
            cv = jnp.where(inside, cv, 0.0)
            pp = jnp.where(inside, pp, 0.0)
        ext_ref[e0:e1, 0:CONV_WIDTH] = cv
        ext_ref[e0:e1, CONV_WIDTH:] = pp

    wconv = wconv_ref[...]
    pscale = pscale_ref[...]
    for r0 in range(0, tm, sub):
        rows = slice(r0, r0 + sub)
        e0 = HALO + r0
        h = h_ref[rows, :]
        conv = (wconv[0:1] * ext_ref[e0 - 1:e0 - 1 + sub, 0:CONV_WIDTH]
                + wconv[1:2] * ext_ref[e0:e0 + sub, 0:CONV_WIDTH]
                + wconv[2:3] * ext_ref[e0 + 1:e0 + 1 + sub, 0:CONV_WIDTH])
        bconv = (_dot(h, win_ref[:, OFF_B:OFF_C]) * conv).astype(BF16)

        t = (i * tm + r0 + lax.broadcasted_iota(jnp.int32, (sub, 1), 0)).astype(F32)
        for q in range(N_GROUPS):
            half = POOL_WINDOWS[q] // 2
            lo = CONV_WIDTH + q * POOL_GROUP
            win = ext_ref[e0 - half:e0 - half + sub, lo:lo + POOL_GROUP]
            for j in range(1 - half, half):
                win = win + ext_ref[e0 + j:e0 + j + sub, lo:lo + POOL_GROUP]
            count = jnp.minimum(t + half, float(SEQ)) - jnp.maximum(t - half, 0.0)
            pooled = (win / count - ext_ref[e0:e0 + sub, lo:lo + POOL_GROUP]).astype(BF16)

            cols = slice(q * OUT_GROUP, (q + 1) * OUT_GROUP)
            gcol = OFF_G + D_MODEL + q * OUT_GROUP
            gate_c = jax.nn.sigmoid(_dot(h, win_ref[:, gcol:gcol + OUT_GROUP]))
            gate_p = jax.nn.sigmoid(_dot(h, win_ref[:, gcol + D_MODEL:gcol + D_MODEL + OUT_GROUP]))
            y_c = _dot(bconv, wco_ref[:, cols])
            y_p = _dot(pooled, wpool_ref[q]) * pscale[:, cols]
            rest_ref[0, rows, cols] = gate_c * y_c + gate_p * y_p


def _mix_in(x, l, g, w_in, wl, w_conv, w_co, w_pool, pscale):
    b, s, d = x.shape
    tm = MIX_TM
    assert tm // MIX_SUB >= 2 and HALO >= POOL_HALF
    nt = s // tm
    hb = tm // HALO
    last_hb = s // HALO - 1
    return pl.pallas_call(
        _mix_in_kernel,
        out_shape=(
            jax.ShapeDtypeStruct((b, s, FOURIER_WIDTH), BF16),
            jax.ShapeDtypeStruct((b, s, d), F32),
        ),
        grid=(b, nt),
        in_specs=[
            pl.BlockSpec((1, tm, d), lambda bi, i: (bi, i, 0)),
            pl.BlockSpec((1, HALO, d), lambda bi, i: (bi, jnp.maximum(i * hb - 1, 0), 0)),
            pl.BlockSpec((1, HALO, d), lambda bi, i: (bi, jnp.minimum((i + 1) * hb, last_hb), 0)),
            _layer(l, (1, d)),
            _layer(wl, (d, IN_WIDTH)),
            _layer(l, (3, CONV_WIDTH)),
            _layer(wl, (CONV_WIDTH, d)),
            pl.BlockSpec((None, N_GROUPS, POOL_GROUP, OUT_GROUP), lambda *_: (wl, 0, 0, 0),
                         pipeline_mode=pl.Buffered(1)),
            _layer(l, (1, d)),
        ],
        out_specs=(
            pl.BlockSpec((1, tm, FOURIER_WIDTH), lambda bi, i: (bi, i, 0)),
            pl.BlockSpec((1, tm, d), lambda bi, i: (bi, i, 0)),
        ),
        scratch_shapes=[pltpu.VMEM((tm + 2 * HALO, CONV_WIDTH + POOL_WIDTH), F32),
                        pltpu.VMEM((tm, d), BF16)],
        compiler_params=pltpu.CompilerParams(
            dimension_semantics=("arbitrary", "arbitrary"), vmem_limit_bytes=VMEM_LIMIT),
        name="mix_in",
    )(x, x, x, g, w_in, w_conv, w_co, w_pool, pscale)


def _fft1_kernel(w_ref, wcc_ref, p_ref, o_ref):
    rows = FFT1_ROWS
    p = p_ref[0].astype(F32)
    a = []
    for s in range(FFT1_SUB):
        pf = p[:, s * SUBLANES:(s + 1) * SUBLANES, :].reshape(rows, FOURIER_WIDTH).astype(BF16)
        a.append(_dot(w_ref[...], pf).astype(BF16))
    tile = (DFT_N1, SUBLANES, FOURIER_GROUP)
    for q in range(N_GROUPS):
        cols = slice(q * FOURIER_GROUP, (q + 1) * FOURIER_GROUP)
        y = [_dot(jnp.concatenate([a_s[:rows, cols], a_s[rows:, cols]], axis=1), wcc_ref[...])
             for a_s in a]
        o_ref[0, 0, :, :, cols] = jnp.concatenate(
            [y_s[:, :FOURIER_GROUP].reshape(tile) for y_s in y], axis=1).astype(BF16)
        o_ref[0, 1, :, :, cols] = jnp.concatenate(
            [y_s[:, FOURIER_GROUP:].reshape(tile) for y_s in y], axis=1).astype(BF16)


def _fft1(p4, w1k, wcc):
    b = p4.shape[0]
    n2_blk = FFT1_SUB * SUBLANES
    return pl.pallas_call(
        _fft1_kernel,
        out_shape=jax.ShapeDtypeStruct((b, 2, DFT_N1, DFT_N2, FOURIER_WIDTH), BF16),
        grid=(b, DFT_N2 // n2_blk),
        in_specs=[
            _resident(w1k.shape),
            _resident(wcc.shape),
            pl.BlockSpec((1, DFT_N1, n2_blk, FOURIER_WIDTH), lambda bi, j: (bi, 0, j, 0)),
        ],
        out_specs=pl.BlockSpec((1, 2, DFT_N1, n2_blk, FOURIER_WIDTH),
                               lambda bi, j: (bi, 0, 0, j, 0)),
        compiler_params=pltpu.CompilerParams(
            dimension_semantics=("arbitrary", "arbitrary"), vmem_limit_bytes=VMEM_LIMIT),
        name="fft1",
    )(w1k, wcc, p4)


def _mix_out_kernel(x_ref, rest_ref, y_ref, gd_ref, g_ref, wga_ref, wgb_ref, wf_ref, wo_ref,
                    o_ref, m_ref, f_ref):
    t = OUT_TOK
    for j in range(SUBLANES):
        yj = jnp.concatenate([y_ref[0, 0, j], y_ref[0, 1, j]], axis=0)
        fj = _dot(gd_ref[0, j], yj)
        for c in range(FOURIER_WIDTH // LANES):
            f_ref[c, pl.ds(j, DFT_N2, stride=SUBLANES), :] = fj[:, c * LANES:(c + 1) * LANES]

    kh = DFT_N2 // 2
    th = t // 2
    for k0 in range(0, DFT_N2, kh):
        rows = slice(k0 * SUBLANES, k0 * SUBLANES + th)
        x = x_ref[0, k0:k0 + kh].reshape(th, D_MODEL)
        h = _rmsnorm_bf16(x, g_ref[...])
        rest = rest_ref[0, k0:k0 + kh].reshape(th, D_MODEL)
        for q in range(N_GROUPS):
            cols = slice(q * OUT_GROUP, (q + 1) * OUT_GROUP)
            wg_ref = wga_ref if q < GATE_BLK // OUT_GROUP else wgb_ref
            gc = (q * OUT_GROUP) % GATE_BLK
            gate_f = jax.nn.sigmoid(_dot(h, wg_ref[:, gc:gc + OUT_GROUP]))
            y_f = _dot(f_ref[q, rows, :].astype(BF16), wf_ref[q])
            m_ref[rows, cols] = (gate_f * y_f + rest[:, cols]).astype(BF16)
        o = x + _dot(m_ref[rows, :], wo_ref[...])
        o_ref[0, k0:k0 + kh] = o.reshape(kh, SUBLANES, D_MODEL)


def _mix_out(x4, rest4, y5, l, gd, g, w_in, wl, wf, wo):
    b = x4.shape[0]
    d = D_MODEL
    tok_blk = (1, DFT_N2, SUBLANES, d)
    tok = lambda ki, bi: (bi, 0, ki, 0)
    gate_col = OFF_G // GATE_BLK
    return pl.pallas_call(
        _mix_out_kernel,
        out_shape=jax.ShapeDtypeStruct(x4.shape, F32),
        grid=(DFT_N1 // SUBLANES, b),
        in_specs=[
            pl.BlockSpec(tok_blk, tok),
            pl.BlockSpec(tok_blk, tok),
            pl.BlockSpec((1, 2, SUBLANES, DFT_N2, FOURIER_WIDTH), lambda ki, bi: (bi, 0, ki, 0, 0)),
            pl.BlockSpec((1, SUBLANES, DFT_N2, 2 * DFT_N2), lambda ki, bi: (ki, 0, 0, 0)),
            _layer(l, (1, d)),
            _layer(wl, (d, GATE_BLK), gate_col),
            _layer(wl, (d, GATE_BLK), gate_col + 1),
            pl.BlockSpec((None, N_GROUPS, FOURIER_GROUP, OUT_GROUP), lambda *_: (wl, 0, 0, 0),
                         pipeline_mode=pl.Buffered(1)),
            _layer(wl, (d, d)),
        ],
        out_specs=pl.BlockSpec(tok_blk, tok),
        scratch_shapes=[pltpu.VMEM((OUT_TOK, d), BF16),
                        pltpu.VMEM((FOURIER_WIDTH // LANES, OUT_TOK, LANES), F32)],
        compiler_params=pltpu.CompilerParams(
            dimension_semantics=("arbitrary", "arbitrary"), vmem_limit_bytes=VMEM_LIMIT),
        name="mix_out",
    )(x4, rest4, y5, gd, g, w_in, w_in, wf, wo)


def _dft_tables():
    two_pi = 2.0 * np.pi
    r8 = SUBLANES
    c = np.arange(FOURIER_GROUP)
    ang = two_pi * ((c[:, None] * c[None, :]) % FOURIER_GROUP) / FOURIER_GROUP
    cs, sn = np.cos(ang), np.sin(ang)
    wcc = np.block([[cs, -sn], [sn, cs]])

    k1 = np.arange(DFT_N1)
    ang = two_pi * ((k1[:, None] * k1[None, :]) % DFT_N1) / DFT_N1
    w1 = np.stack([np.cos(ang), -np.sin(ang)], axis=0)
    eye = np.eye(r8)
    w1k = (w1[:, :, None, :, None] * eye[None, None, :, None, :]).reshape(2 * FFT1_ROWS, FFT1_ROWS)

    nkb = DFT_N1 // r8
    kk = (r8 * np.arange(nkb)[:, None, None] + np.arange(r8)[None, :, None]
          + DFT_N1 * np.arange(DFT_N2)[None, None, :])
    ang = two_pi * ((kk[..., None] * np.arange(DFT_N2)) % SEQ) / SEQ
    norm = 1.0 / math.sqrt(FOURIER_GROUP * SEQ)
    gd = (np.stack([np.cos(ang), np.sin(ang)], axis=3) * norm).reshape(nkb, r8, DFT_N2, 2 * DFT_N2)
    as_bf16 = lambda a: jnp.asarray(a.astype(np.float32)).astype(BF16)
    return as_bf16(wcc), as_bf16(w1k), as_bf16(gd)


def kernel(x, g_ffn1, w1_a, w3_a, w2_a, g_mix, w_in, w_fourier, w_conv, w_conv_out,
           w_pool, pool_scale, w_o, g_ffn2, w1_b, w3_b, w2_b, g_final):
    b, s, d = x.shape
    assert (s, d) == (SEQ, D_MODEL)
    wcc, w1k, gd = _dft_tables()
    row = lambda v: v.reshape(DEPTH, 1, d)
    gfin = g_final.reshape(1, d)
    g_ffn1, g_mix, g_ffn2, pool_scale = row(g_ffn1), row(g_mix), row(g_ffn2), row(pool_scale)
    grouped = (DEPTH, N_GROUPS * FOURIER_GROUP, OUT_GROUP)
    w_fourier, w_pool = w_fourier.reshape(grouped), w_pool.reshape(grouped)
    maps = (1, N_GROUPS, FOURIER_GROUP, OUT_GROUP)
    view = (b, DFT_N2, DFT_N1, d)
    ffn_a = (w1_a[:1].astype(BF16), w3_a[:1].astype(BF16), w2_a[:1].astype(BF16), 0)
    x = x.reshape(b * s, d)
    for l in range(DEPTH):
        x, (w_in_l, w1, w3, w2, w_o_l, w_co_l, w_f_l, w_p_l) = _ffn(
            x, l, g_ffn1, ffn_a, gfin, False,
            cast=((w_in, l), (w1_b, l), (w3_b, l), (w2_b, l),
                  (w_o, l), (w_conv_out, l), (w_fourier, l), (w_pool, l)))
        ffn_b = (w1, w3, w2, 0)
        x = x.reshape(b, s, d)
        pf, rest = _mix_in(x, l, g_mix, w_in_l, 0, w_conv, w_co_l, w_p_l.reshape(maps), pool_scale)
        y = _fft1(pf.reshape(b, DFT_N1, DFT_N2, FOURIER_WIDTH), w1k, wcc)
        x = _mix_out(x.reshape(view), rest.reshape(view), y, l, gd, g_mix, w_in_l, 0,
                     w_f_l.reshape(maps), w_o_l).reshape(b * s, d)
        nxt = l + 1
        cast = ((w1_a, nxt), (w3_a, nxt), (w2_a, nxt)) if nxt < DEPTH else ()
        x, cast_out = _ffn(x, l, g_ffn2, ffn_b, gfin, nxt == DEPTH, cast=cast)
        if cast_out:
            ffn_a = tuple(cast_out) + (0,)
    return x.reshape(b, s, d)
```

```python
import functools
import math

import numpy as np
import jax
import jax.numpy as jnp
from jax import lax
from jax.experimental import pallas as pl
from jax.experimental.pallas import tpu as pltpu

D_MODEL = 1024
SEQ = 8192
DEPTH = 4
N_GROUPS = 4
FOURIER_WIDTH = 512
FOURIER_GROUP = FOURIER_WIDTH // N_GROUPS
CONV_WIDTH = 512
POOL_WIDTH = 512
POOL_GROUP = POOL_WIDTH // N_GROUPS
POOL_WINDOWS = (2, 4, 8, 16)
OUT_GROUP = D_MODEL // N_GROUPS
D_FF = 2816
EPS = 1e-6

OFF_F = 0
OFF_B = OFF_F + FOURIER_WIDTH
OFF_C = OFF_B + CONV_WIDTH
OFF_V = OFF_C + CONV_WIDTH
OFF_P = OFF_V + CONV_WIDTH
OFF_G = OFF_P + POOL_WIDTH
IN_WIDTH = OFF_G + 3 * D_MODEL

SUBLANES = 8
LANES = 128
MXU_COLS = 256
VMEM_LIMIT = 56 * 1024 * 1024

DFT_N1 = 64
DFT_N2 = SEQ // DFT_N1
FFT1_ROWS = DFT_N1 * SUBLANES
OUT_TOK = DFT_N2 * SUBLANES
GATE_BLK = 512

BF16_ROWS = 16
CAST_ROWS = 128
POOL_HALF = max(POOL_WINDOWS) // 2
HALO = BF16_ROWS
FFN_TM = 1024
FFN_SUB = 256
FFT1_SUB = 8
MIX_TM = 1024
MIX_SUB = 512
RING = 3

BF16 = jnp.bfloat16
F32 = jnp.float32


def _dot(a, b):
    return jnp.dot(a, b, preferred_element_type=F32)


def _rmsnorm_bf16(x, g):
    ms = jnp.mean(x * x, axis=-1, keepdims=True)
    return (x * lax.rsqrt(ms + EPS) * g).astype(BF16)


def _resident(shape):
    nd = len(shape)
    return pl.BlockSpec(shape, lambda *_: (0,) * nd, pipeline_mode=pl.Buffered(1))


def _layer(l, shape, col=0):
    return pl.BlockSpec((None,) + shape, lambda *_: (l, 0, col), pipeline_mode=pl.Buffered(1))


def _ffn_kernel(*refs, final_norm, n_cast):
    x_ref, g_ref, w1_ref, w3_ref, w2_ref, gfin_ref = refs[:6]
    cast_in = refs[6:6 + n_cast]
    o_ref = refs[6 + n_cast]
    cast_out = refs[7 + n_cast:7 + 2 * n_cast]
    act_ref = refs[7 + 2 * n_cast]

    for src, dst in zip(cast_in, cast_out):
        dst[...] = src[...].astype(BF16)

    for r0 in range(0, FFN_TM, FFN_SUB):
        rows = slice(r0, r0 + FFN_SUB)
        x = x_ref[rows, :]
        h = _rmsnorm_bf16(x, g_ref[...])
        for c in range(0, D_FF, MXU_COLS):
            a = _dot(h, w1_ref[:, c:c + MXU_COLS])
            b = _dot(h, w3_ref[:, c:c + MXU_COLS])
            act_ref[rows, c:c + MXU_COLS] = (jax.nn.silu(a) * b).astype(BF16)
        y = x + 0.5 * _dot(act_ref[rows, :], w2_ref[...])
        if final_norm:
            ms = jnp.mean(y * y, axis=-1, keepdims=True)
            y = y * lax.rsqrt(ms + EPS) * gfin_ref[...]
        o_ref[rows, :] = y


def _cast_chunks(rows, steps):
    if rows % steps == 0 and (rows // steps) % BF16_ROWS == 0:
        return rows // steps, steps
    assert rows % CAST_ROWS == 0 and rows // CAST_ROWS <= steps
    return CAST_ROWS, rows // CAST_ROWS


def _ffn(x2d, l, g, weights, gfin, final_norm, cast=()):
    t, d = x2d.shape
    tm = FFN_TM
    steps = t // tm
    w1, w3, w2, wl = weights
    cast_in_specs, cast_out_specs, cast_out_shapes = [], [], []
    for stack, cl in cast:
        _, rows, cols = stack.shape
        chunk, n = _cast_chunks(rows, steps)
        cast_in_specs.append(pl.BlockSpec(
            (None, chunk, cols), lambda i, cl=cl, n=n: (cl, jnp.minimum(i, n - 1), 0)))
        cast_out_specs.append(pl.BlockSpec(
            (None, chunk, cols), lambda i, n=n: (0, jnp.minimum(i, n - 1), 0)))
        cast_out_shapes.append(jax.ShapeDtypeStruct((1, rows, cols), BF16))
    n_cast = len(cast)
    outs = pl.pallas_call(
        functools.partial(_ffn_kernel, final_norm=final_norm, n_cast=n_cast),
        out_shape=[jax.ShapeDtypeStruct((t, d), F32)] + cast_out_shapes,
        grid=(steps,),
        in_specs=[
            pl.BlockSpec((tm, d), lambda i: (i, 0)),
            _layer(l, (1, d)),
            _layer(wl, (d, D_FF)),
            _layer(wl, (d, D_FF)),
            _layer(wl, (D_FF, d)),
            _resident((1, d)),
        ] + cast_in_specs,
        out_specs=[pl.BlockSpec((tm, d), lambda i: (i, 0))] + cast_out_specs,
        scratch_shapes=[pltpu.VMEM((tm, D_FF), BF16)],
        compiler_params=pltpu.CompilerParams(
            dimension_semantics=("arbitrary",), vmem_limit_bytes=VMEM_LIMIT),
        name="ffn_final" if final_norm else "ffn",
    )(x2d, g, w1, w3, w2, gfin, *[stack for stack, _ in cast])
    return outs[0], outs[1:]


def _mix_in_kernel(x_ref, xp_ref, xn_ref, g_ref, win_ref, wconv_ref, wco_ref,
                   wpool_ref, pscale_ref, pf_ref, rest_ref, ext_ref, h_ref):
    tm = MIX_TM
    sub = MIX_SUB
    i = pl.program_id(1)
    nt = pl.num_programs(1)
    g = g_ref[...]

    last = tm - sub
    for r0 in range(0, tm, sub):
        rows = slice(r0, r0 + sub)
        h = _rmsnorm_bf16(x_ref[0, rows, :], g)
        h_ref[rows, :] = h
        pf_ref[0, rows, :] = _dot(h, win_ref[:, OFF_F:OFF_B]).astype(BF16)
        e0, e1 = HALO + r0, HALO + r0 + sub
        if r0 == 0:
            h = jnp.concatenate([_rmsnorm_bf16(xp_ref[0], g), h], axis=0)
            e0 -= HALO
        if r0 == last:
            h = jnp.concatenate([h, _rmsnorm_bf16(xn_ref[0], g)], axis=0)
            e1 += HALO
        n = e1 - e0
        row = lax.broadcasted_iota(jnp.int32, (n, 1), 0)
        inside = None
        if r0 == 0:
            inside = (row >= HALO) | (i > 0)
        if r0 == last:
            ok = (row < n - HALO) | (i < nt - 1)
            inside = ok if inside is None else inside & ok
        cv = _dot(h, win_ref[:, OFF_C:OFF_V]) * _dot(h, win_ref[:, OFF_V:OFF_P])
        pp = _dot(h, win_ref[:, OFF_P:OFF_G])
        if inside is not None:
            cv = jnp.where(inside, cv, 0.0)
            pp = jnp.where(inside, pp, 0.0)
        ext_ref[e0:e1, 0:CONV_WIDTH] = cv
        ext_ref[e0:e1, CONV_WIDTH:] = pp

    wconv = wconv_ref[...]
    pscale = pscale_ref[...]
    for r0 in range(0, tm, sub):
        rows = slice(r0, r0 + sub)
        e0 = HALO + r0
        h = h_ref[rows, :]
        conv = (wconv[0:1] * ext_ref[e0 - 1:e0 - 1 + sub, 0:CONV_WIDTH]
                + wconv[1:2] * ext_ref[e0:e0 + sub, 0:CONV_WIDTH]
                + wconv[2:3] * ext_ref[e0 + 1:e0 + 1 + sub, 0:CONV_WIDTH])
        bconv = (_dot(h, win_ref[:, OFF_B:OFF_C]) * conv).astype(BF16)

        t = (i * tm + r0 + lax.broadcasted_iota(jnp.int32, (sub, 1), 0)).astype(F32)
        for q in range(N_GROUPS):
            half = POOL_WINDOWS[q] // 2
            lo = CONV_WIDTH + q * POOL_GROUP
            win = ext_ref[e0 - half:e0 - half + sub, lo:lo + POOL_GROUP]
            for j in range(1 - half, half):
                win = win + ext_ref[e0 + j:e0 + j + sub, lo:lo + POOL_GROUP]
            count = jnp.minimum(t + half, float(SEQ)) - jnp.maximum(t - half, 0.0)
            pooled = (win / count - ext_ref[e0:e0 + sub, lo:lo + POOL_GROUP]).astype(BF16)

            cols = slice(q * OUT_GROUP, (q + 1) * OUT_GROUP)
            gcol = OFF_G + D_MODEL + q * OUT_GROUP
            gate_c = jax.nn.sigmoid(_dot(h, win_ref[:, gcol:gcol + OUT_GROUP]))
            gate_p = jax.nn.sigmoid(_dot(h, win_ref[:, gcol + D_MODEL:gcol + D_MODEL + OUT_GROUP]))
            y_c = _dot(bconv, wco_ref[:, cols])
            y_p = _dot(pooled, wpool_ref[q]) * pscale[:, cols]
            rest_ref[0, rows, cols] = gate_c * y_c + gate_p * y_p


def _mix_in(x, l, g, w_in, wl, w_conv, w_co, w_pool, pscale):
    b, s, d = x.shape
    tm = MIX_TM
    assert tm // MIX_SUB >= 2 and HALO >= POOL_HALF
    nt = s // tm
    hb = tm // HALO
    last_hb = s // HALO - 1
    return pl.pallas_call(
        _mix_in_kernel,
        out_shape=(
            jax.ShapeDtypeStruct((b, s, FOURIER_WIDTH), BF16),
            jax.ShapeDtypeStruct((b, s, d), F32),
        ),
        grid=(b, nt),
        in_specs=[
            pl.BlockSpec((1, tm, d), lambda bi, i: (bi, i, 0)),
            pl.BlockSpec((1, HALO, d), lambda bi, i: (bi, jnp.maximum(i * hb - 1, 0), 0)),
            pl.BlockSpec((1, HALO, d), lambda bi, i: (bi, jnp.minimum((i + 1) * hb, last_hb), 0)),
            _layer(l, (1, d)),
            _layer(wl, (d, IN_WIDTH)),
            _layer(l, (3, CONV_WIDTH)),
            _layer(wl, (CONV_WIDTH, d)),
            pl.BlockSpec((None, N_GROUPS, POOL_GROUP, OUT_GROUP), lambda *_: (wl, 0, 0, 0),
                         pipeline_mode=pl.Buffered(1)),
            _layer(l, (1, d)),
        ],
        out_specs=(
            pl.BlockSpec((1, tm, FOURIER_WIDTH), lambda bi, i: (bi, i, 0)),
            pl.BlockSpec((1, tm, d), lambda bi, i: (bi, i, 0)),
        ),
        scratch_shapes=[pltpu.VMEM((tm + 2 * HALO, CONV_WIDTH + POOL_WIDTH), F32),
                        pltpu.VMEM((tm, d), BF16)],
        compiler_params=pltpu.CompilerParams(
            dimension_semantics=("arbitrary", "arbitrary"), vmem_limit_bytes=VMEM_LIMIT),
        name="mix_in",
    )(x, x, x, g, w_in, w_conv, w_co, w_pool, pscale)


def _fft1_kernel(w_ref, wcc_ref, p_ref, o_ref):
    rows = FFT1_ROWS
    p = p_ref[0].astype(F32)
    a = []
    for s in range(FFT1_SUB):
        pf = p[:, s * SUBLANES:(s + 1) * SUBLANES, :].reshape(rows, FOURIER_WIDTH).astype(BF16)
        a.append(_dot(w_ref[...], pf).astype(BF16))
    tile = (DFT_N1, SUBLANES, FOURIER_GROUP)
    for q in range(N_GROUPS):
        cols = slice(q * FOURIER_GROUP, (q + 1) * FOURIER_GROUP)
        y = [_dot(jnp.concatenate([a_s[:rows, cols], a_s[rows:, cols]], axis=1), wcc_ref[...])
             for a_s in a]
        o_ref[0, 0, :, :, cols] = jnp.concatenate(
            [y_s[:, :FOURIER_GROUP].reshape(tile) for y_s in y], axis=1).astype(BF16)
        o_ref[0, 1, :, :, cols] = jnp.concatenate(
            [y_s[:, FOURIER_GROUP:].reshape(tile) for y_s in y], axis=1).astype(BF16)


def _fft1(p4, w1k, wcc):
    b = p4.shape[0]
    n2_blk = FFT1_SUB * SUBLANES
    return pl.pallas_call(
        _fft1_kernel,
        out_shape=jax.ShapeDtypeStruct((b, 2, DFT_N1, DFT_N2, FOURIER_WIDTH), BF16),
        grid=(b, DFT_N2 // n2_blk),
        in_specs=[
            _resident(w1k.shape),
            _resident(wcc.shape),
            pl.BlockSpec((1, DFT_N1, n2_blk, FOURIER_WIDTH), lambda bi, j: (bi, 0, j, 0)),
        ],
        out_specs=pl.BlockSpec((1, 2, DFT_N1, n2_blk, FOURIER_WIDTH),
                               lambda bi, j: (bi, 0, 0, j, 0)),
        compiler_params=pltpu.CompilerParams(
            dimension_semantics=("arbitrary", "arbitrary"), vmem_limit_bytes=VMEM_LIMIT),
        name="fft1",
    )(w1k, wcc, p4)


def _ring_copies(x_hbm, rest_hbm, xbuf, rbuf, sem, step, n_batch):
    ki = step // n_batch
    bi = step % n_batch
    slot = step % RING
    k1 = pl.ds(ki * SUBLANES, SUBLANES)
    return (pltpu.make_async_copy(x_hbm.at[bi, :, k1, :], xbuf.at[slot], sem.at[0, slot]),
            pltpu.make_async_copy(rest_hbm.at[bi, :, k1, :], rbuf.at[slot], sem.at[1, slot]))


def _mix_out_kernel(x_hbm, rest_hbm, y_ref, gd_ref, g_ref, wga_ref, wgb_ref, wf_ref, wo_ref,
                    o_ref, m_ref, f_ref, xbuf, rbuf, sem):
    t = OUT_TOK
    n_batch = pl.num_programs(1)
    step = pl.program_id(0) * n_batch + pl.program_id(1)
    n_steps = pl.num_programs(0) * n_batch
    ring = functools.partial(_ring_copies, x_hbm, rest_hbm, xbuf, rbuf, sem, n_batch=n_batch)

    @pl.when(step == 0)
    def _():
        for s in range(RING - 1):
            for cp in ring(step=s):
                cp.start()

    @pl.when(step + RING - 1 < n_steps)
    def _():
        for cp in ring(step=step + RING - 1):
            cp.start()

    for j in range(SUBLANES):
        yj = jnp.concatenate([y_ref[0, 0, j], y_ref[0, 1, j]], axis=0)
        fj = _dot(gd_ref[0, j], yj)
        for c in range(FOURIER_WIDTH // LANES):
            f_ref[c, pl.ds(j, DFT_N2, stride=SUBLANES), :] = fj[:, c * LANES:(c + 1) * LANES]

    for cp in ring(step=step):
        cp.wait()
    slot = step % RING
    kh = DFT_N2 // 2
    th = t // 2
    for k0 in range(0, DFT_N2, kh):
        rows = slice(k0 * SUBLANES, k0 * SUBLANES + th)
        x = xbuf[slot, k0:k0 + kh].reshape(th, D_MODEL)
        h = _rmsnorm_bf16(x, g_ref[...])
        rest = rbuf[slot, k0:k0 + kh].reshape(th, D_MODEL)
        for q in range(N_GROUPS):
            cols = slice(q * OUT_GROUP, (q + 1) * OUT_GROUP)
            wg_ref = wga_ref if q < GATE_BLK // OUT_GROUP else wgb_ref
            gc = (q * OUT_GROUP) % GATE_BLK
            gate_f = jax.nn.sigmoid(_dot(h, wg_ref[:, gc:gc + OUT_GROUP]))
            y_f = _dot(f_ref[q, rows, :].astype(BF16), wf_ref[q])
            m_ref[rows, cols] = (gate_f * y_f + rest[:, cols]).astype(BF16)
        o = x + _dot(m_ref[rows, :], wo_ref[...])
        o_ref[0, k0:k0 + kh] = o.reshape(kh, SUBLANES, D_MODEL)


def _mix_out(x4, rest4, y5, l, gd, g, w_in, wl, wf, wo):
    b = x4.shape[0]
    d = D_MODEL
    tok_blk = (1, DFT_N2, SUBLANES, d)
    tok = lambda ki, bi: (bi, 0, ki, 0)
    gate_col = OFF_G // GATE_BLK
    return pl.pallas_call(
        _mix_out_kernel,
        out_shape=jax.ShapeDtypeStruct(x4.shape, F32),
        grid=(DFT_N1 // SUBLANES, b),
        in_specs=[
            pl.BlockSpec(memory_space=pl.ANY),
            pl.BlockSpec(memory_space=pl.ANY),
            pl.BlockSpec((1, 2, SUBLANES, DFT_N2, FOURIER_WIDTH), lambda ki, bi: (bi, 0, ki, 0, 0)),
            pl.BlockSpec((1, SUBLANES, DFT_N2, 2 * DFT_N2), lambda ki, bi: (ki, 0, 0, 0)),
            _layer(l, (1, d)),
            _layer(wl, (d, GATE_BLK), gate_col),
            _layer(wl, (d, GATE_BLK), gate_col + 1),
            pl.BlockSpec((None, N_GROUPS, FOURIER_GROUP, OUT_GROUP), lambda *_: (wl, 0, 0, 0),
                         pipeline_mode=pl.Buffered(1)),
            _layer(wl, (d, d)),
        ],
        out_specs=pl.BlockSpec(tok_blk, tok),
        scratch_shapes=[pltpu.VMEM((OUT_TOK, d), BF16),
                        pltpu.VMEM((FOURIER_WIDTH // LANES, OUT_TOK, LANES), F32),
                        pltpu.VMEM((RING, DFT_N2, SUBLANES, d), F32),
                        pltpu.VMEM((RING, DFT_N2, SUBLANES, d), F32),
                        pltpu.SemaphoreType.DMA((2, RING))],
        compiler_params=pltpu.CompilerParams(
            dimension_semantics=("arbitrary", "arbitrary"), vmem_limit_bytes=VMEM_LIMIT),
        name="mix_out",
    )(x4, rest4, y5, gd, g, w_in, w_in, wf, wo)


def _dft_tables():
    two_pi = 2.0 * np.pi
    r8 = SUBLANES
    c = np.arange(FOURIER_GROUP)
    ang = two_pi * ((c[:, None] * c[None, :]) % FOURIER_GROUP) / FOURIER_GROUP
    cs, sn = np.cos(ang), np.sin(ang)
    wcc = np.block([[cs, -sn], [sn, cs]])

    k1 = np.arange(DFT_N1)
    ang = two_pi * ((k1[:, None] * k1[None, :]) % DFT_N1) / DFT_N1
    w1 = np.stack([np.cos(ang), -np.sin(ang)], axis=0)
    eye = np.eye(r8)
    w1k = (w1[:, :, None, :, None] * eye[None, None, :, None, :]).reshape(2 * FFT1_ROWS, FFT1_ROWS)

    nkb = DFT_N1 // r8
    kk = (r8 * np.arange(nkb)[:, None, None] + np.arange(r8)[None, :, None]
          + DFT_N1 * np.arange(DFT_N2)[None, None, :])
    ang = two_pi * ((kk[..., None] * np.arange(DFT_N2)) % SEQ) / SEQ
    norm = 1.0 / math.sqrt(FOURIER_GROUP * SEQ)
    gd = (np.stack([np.cos(ang), np.sin(ang)], axis=3) * norm).reshape(nkb, r8, DFT_N2, 2 * DFT_N2)
    as_bf16 = lambda a: jnp.asarray(a.astype(np.float32)).astype(BF16)
    return as_bf16(wcc), as_bf16(w1k), as_bf16(gd)


def kernel(x, g_ffn1, w1_a, w3_a, w2_a, g_mix, w_in, w_fourier, w_conv, w_conv_out,
           w_pool, pool_scale, w_o, g_ffn2, w1_b, w3_b, w2_b, g_final):
    b, s, d = x.shape
    assert (s, d) == (SEQ, D_MODEL)
    wcc, w1k, gd = _dft_tables()
    row = lambda v: v.reshape(DEPTH, 1, d)
    gfin = g_final.reshape(1, d)
    g_ffn1, g_mix, g_ffn2, pool_scale = row(g_ffn1), row(g_mix), row(g_ffn2), row(pool_scale)
    grouped = (DEPTH, N_GROUPS * FOURIER_GROUP, OUT_GROUP)
    w_fourier, w_pool = w_fourier.reshape(grouped), w_pool.reshape(grouped)
    maps = (1, N_GROUPS, FOURIER_GROUP, OUT_GROUP)
    view = (b, DFT_N2, DFT_N1, d)
    ffn_a = (w1_a[:1].astype(BF16), w3_a[:1].astype(BF16), w2_a[:1].astype(BF16), 0)
    x = x.reshape(b * s, d)
    for l in range(DEPTH):
        x, (w_in_l, w1, w3, w2, w_o_l, w_co_l, w_f_l, w_p_l) = _ffn(
            x, l, g_ffn1, ffn_a, gfin, False,
            cast=((w_in, l), (w1_b, l), (w3_b, l), (w2_b, l),
                  (w_o, l), (w_conv_out, l), (w_fourier, l), (w_pool, l)))
        ffn_b = (w1, w3, w2, 0)
        x = x.reshape(b, s, d)
        pf, rest = _mix_in(x, l, g_mix, w_in_l, 0, w_conv, w_co_l, w_p_l.reshape(maps), pool_scale)
        y = _fft1(pf.reshape(b, DFT_N1, DFT_N2, FOURIER_WIDTH), w1k, wcc)
        x = _mix_out(x.reshape(view), rest.reshape(view), y, l, gd, g_mix, w_in_l, 0,
                     w_f_l.reshape(maps), w_o_l).reshape(b * s, d)
        nxt = l + 1
        cast = ((w1_a, nxt), (w3_a, nxt), (w2_a, nxt)) if nxt < DEPTH else ()
        x, cast_out = _ffn(x, l, g_ffn2, ffn_b, gfin, nxt == DEPTH, cast=cast)
        if cast_out:
            ffn_a = tuple(cast_out) + (0,)
    return x.reshape(b, s, d)
```

```python
import functools
import math

import numpy as np
import jax
import jax.numpy as jnp
from jax import lax
from jax.experimental import pallas as pl
from jax.experimental.pallas import tpu as pltpu

D_MODEL = 1024
SEQ = 8192
DEPTH = 4
N_GROUPS = 4
FOURIER_WIDTH = 512
FOURIER_GROUP = FOURIER_WIDTH // N_GROUPS
CONV_WIDTH = 512
POOL_WIDTH = 512
POOL_GROUP = POOL_WIDTH // N_GROUPS
POOL_WINDOWS = (2, 4, 8, 16)
OUT_GROUP = D_MODEL // N_GROUPS
D_FF = 2816
EPS = 1e-6

OFF_F = 0
OFF_B = OFF_F + FOURIER_WIDTH
OFF_C = OFF_B + CONV_WIDTH
OFF_V = OFF_C + CONV_WIDTH
OFF_P = OFF_V + CONV_WIDTH
OFF_G = OFF_P + POOL_WIDTH
IN_WIDTH = OFF_G + 3 * D_MODEL

SUBLANES = 8
LANES = 128
MXU_COLS = 256
VMEM_LIMIT = 56 * 1024 * 1024

DFT_N1 = 64
DFT_N2 = SEQ // DFT_N1
FFT1_ROWS = DFT_N1 * SUBLANES
OUT_TOK = DFT_N2 * SUBLANES
GATE_BLK = 512

BF16_ROWS = 16
CAST_ROWS = 128
POOL_HALF = max(POOL_WINDOWS) // 2
HALO = BF16_ROWS
FFN_TM = 1024
FFN_SUB = 256
FFT1_SUB = 8
MIX_TM = 1024
MIX_SUB = 512

BF16 = jnp.bfloat16
F32 = jnp.float32


def _dot(a, b):
    return jnp.dot(a, b, preferred_element_type=F32)


def _rmsnorm_bf16(x, g):
    ms = jnp.mean(x * x, axis=-1, keepdims=True)
    return (x * lax.rsqrt(ms + EPS) * g).astype(BF16)


def _resident(shape):
    nd = len(shape)
    return pl.BlockSpec(shape, lambda *_: (0,) * nd, pipeline_mode=pl.Buffered(1))


def _layer(l, shape, col=0):
    return pl.BlockSpec((None,) + shape, lambda *_: (l, 0, col), pipeline_mode=pl.Buffered(1))


def _ffn_kernel(*refs, final_norm, n_cast):
    x_ref, g_ref, w1_ref, w3_ref, w2_ref, gfin_ref = refs[:6]
    cast_in = refs[6:6 + n_cast]
    o_ref = refs[6 + n_cast]
    cast_out = refs[7 + n_cast:7 + 2 * n_cast]
    act_ref = refs[7 + 2 * n_cast]

    for src, dst in zip(cast_in, cast_out):
        dst[...] = src[...].astype(BF16)

    for r0 in range(0, FFN_TM, FFN_SUB):
        rows = slice(r0, r0 + FFN_SUB)
        x = x_ref[rows, :]
        h = _rmsnorm_bf16(x, g_ref[...])
        for c in range(0, D_FF, MXU_COLS):
            a = _dot(h, w1_ref[:, c:c + MXU_COLS])
            b = _dot(h, w3_ref[:, c:c + MXU_COLS])
            act_ref[rows, c:c + MXU_COLS] = (jax.nn.silu(a) * b).astype(BF16)
        y = x + 0.5 * _dot(act_ref[rows, :], w2_ref[...])
        if final_norm:
            ms = jnp.mean(y * y, axis=-1, keepdims=True)
            y = y * lax.rsqrt(ms + EPS) * gfin_ref[...]
        o_ref[rows, :] = y


def _cast_chunks(rows, steps):
    if rows % steps == 0 and (rows // steps) % BF16_ROWS == 0:
        return rows // steps, steps
    assert rows % CAST_ROWS == 0 and rows // CAST_ROWS <= steps
    return CAST_ROWS, rows // CAST_ROWS


def _ffn(x2d, l, g, weights, gfin, final_norm, cast=()):
    t, d = x2d.shape
    tm = FFN_TM
    steps = t // tm
    w1, w3, w2, wl = weights
    cast_in_specs, cast_out_specs, cast_out_shapes = [], [], []
    for stack, cl in cast:
        _, rows, cols = stack.shape
        chunk, n = _cast_chunks(rows, steps)
        cast_in_specs.append(pl.BlockSpec(
            (None, chunk, cols), lambda i, cl=cl, n=n: (cl, jnp.minimum(i, n - 1), 0)))
        cast_out_specs.append(pl.BlockSpec(
            (None, chunk, cols), lambda i, n=n: (0, jnp.minimum(i, n - 1), 0)))
        cast_out_shapes.append(jax.ShapeDtypeStruct((1, rows, cols), BF16))
    n_cast = len(cast)
    outs = pl.pallas_call(
        functools.partial(_ffn_kernel, final_norm=final_norm, n_cast=n_cast),
        out_shape=[jax.ShapeDtypeStruct((t, d), F32)] + cast_out_shapes,
        grid=(steps,),
        in_specs=[
            pl.BlockSpec((tm, d), lambda i: (i, 0)),
            _layer(l, (1, d)),
            _layer(wl, (d, D_FF)),
            _layer(wl, (d, D_FF)),
            _layer(wl, (D_FF, d)),
            _resident((1, d)),
        ] + cast_in_specs,
        out_specs=[pl.BlockSpec((tm, d), lambda i: (i, 0))] + cast_out_specs,
        scratch_shapes=[pltpu.VMEM((tm, D_FF), BF16)],
        compiler_params=pltpu.CompilerParams(
            dimension_semantics=("arbitrary",), vmem_limit_bytes=VMEM_LIMIT),
        name="ffn_final" if final_norm else "ffn",
    )(x2d, g, w1, w3, w2, gfin, *[stack for stack, _ in cast])
    return outs[0], outs[1:]


def _mix_in_kernel(x_ref, xp_ref, xn_ref, g_ref, win_ref, wconv_ref, wco_ref,
                   wpool_ref, pscale_ref, pf_ref, rest_ref, ext_ref, h_ref):
    tm = MIX_TM
    sub = MIX_SUB
    i = pl.program_id(1)
    nt = pl.num_programs(1)
    g = g_ref[...]

    last = tm - sub
    for r0 in range(0, tm, sub):
        rows = slice(r0, r0 + sub)
        h = _rmsnorm_bf16(x_ref[0, rows, :], g)
        h_ref[rows, :] = h
        pf_ref[0, rows, :] = _dot(h, win_ref[:, OFF_F:OFF_B]).astype(BF16)
        e0, e1 = HALO + r0, HALO + r0 + sub
        if r0 == 0:
            h = jnp.concatenate([_rmsnorm_bf16(xp_ref[0], g), h], axis=0)
            e0 -= HALO
        if r0 == last:
            h = jnp.concatenate([h, _rmsnorm_bf16(xn_ref[0], g)], axis=0)
            e1 += HALO
        n = e1 - e0
        row = lax.broadcasted_iota(jnp.int32, (n, 1), 0)
        inside = None
        if r0 == 0:
            inside = (row >= HALO) | (i > 0)
        if r0 == last:
            ok = (row < n - HALO) | (i < nt - 1)
            inside = ok if inside is None else inside & ok
        cv = _dot(h, win_ref[:, OFF_C:OFF_V]) * _dot(h, win_ref[:, OFF_V:OFF_P])
        pp = _dot(h, win_ref[:, OFF_P:OFF_G])
        if inside is not None:
            cv = jnp.where(inside, cv, 0.0)
            pp = jnp.where(inside, pp, 0.0)
        ext_ref[e0:e1, 0:CONV_WIDTH] = cv
        ext_ref[e0:e1, CONV_WIDTH:] = pp

    wconv = wconv_ref[...]
    pscale = pscale_ref[...]
    for r0 in range(0, tm, sub):
        rows = slice(r0, r0 + sub)
        e0 = HALO + r0
        h = h_ref[rows, :]
        conv = (wconv[0:1] * ext_ref[e0 - 1:e0 - 1 + sub, 0:CONV_WIDTH]
                + wconv[1:2] * ext_ref[e0:e0 + sub, 0:CONV_WIDTH]
                + wconv[2:3] * ext_ref[e0 + 1:e0 + 1 + sub, 0:CONV_WIDTH])
        bconv = (_dot(h, win_ref[:, OFF_B:OFF_C]) * conv).astype(BF16)

        t = (i * tm + r0 + lax.broadcasted_iota(jnp.int32, (sub, 1), 0)).astype(F32)
        for q in range(N_GROUPS):
            half = POOL_WINDOWS[q] // 2
            lo = CONV_WIDTH + q * POOL_GROUP
            win = ext_ref[e0 - half:e0 - half + sub, lo:lo + POOL_GROUP]
            for j in range(1 - half, half):
                win = win + ext_ref[e0 + j:e0 + j + sub, lo:lo + POOL_GROUP]
            count = jnp.minimum(t + half, float(SEQ)) - jnp.maximum(t - half, 0.0)
            pooled = (win / count - ext_ref[e0:e0 + sub, lo:lo + POOL_GROUP]).astype(BF16)

            cols = slice(q * OUT_GROUP, (q + 1) * OUT_GROUP)
            gcol = OFF_G + D_MODEL + q * OUT_GROUP
            gate_c = jax.nn.sigmoid(_dot(h, win_ref[:, gcol:gcol + OUT_GROUP]))
            gate_p = jax.nn.sigmoid(_dot(h, win_ref[:, gcol + D_MODEL:gcol + D_MODEL + OUT_GROUP]))
            y_c = _dot(bconv, wco_ref[:, cols])
            y_p = _dot(pooled, wpool_ref[q]) * pscale[:, cols]
            rest_ref[0, rows, cols] = gate_c * y_c + gate_p * y_p


def _mix_in(x, l, g, w_in, wl, w_conv, w_co, w_pool, pscale):
    b, s, d = x.shape
    tm = MIX_TM
    assert tm // MIX_SUB >= 2 and HALO >= POOL_HALF
    nt = s // tm
    hb = tm // HALO
    last_hb = s // HALO - 1
    return pl.pallas_call(
        _mix_in_kernel,
        out_shape=(
            jax.ShapeDtypeStruct((b, s, FOURIER_WIDTH), BF16),
            jax.ShapeDtypeStruct((b, s, d), F32),
        ),
        grid=(b, nt),
        in_specs=[
            pl.BlockSpec((1, tm, d), lambda bi, i: (bi, i, 0)),
            pl.BlockSpec((1, HALO, d), lambda bi, i: (bi, jnp.maximum(i * hb - 1, 0), 0)),
            pl.BlockSpec((1, HALO, d), lambda bi, i: (bi, jnp.minimum((i + 1) * hb, last_hb), 0)),
            _layer(l, (1, d)),
            _layer(wl, (d, IN_WIDTH)),
            _layer(l, (3, CONV_WIDTH)),
            _layer(wl, (CONV_WIDTH, d)),
            pl.BlockSpec((None, N_GROUPS, POOL_GROUP, OUT_GROUP), lambda *_: (wl, 0, 0, 0),
                         pipeline_mode=pl.Buffered(1)),
            _layer(l, (1, d)),
        ],
        out_specs=(
            pl.BlockSpec((1, tm, FOURIER_WIDTH), lambda bi, i: (bi, i, 0)),
            pl.BlockSpec((1, tm, d), lambda bi, i: (bi, i, 0)),
        ),
        scratch_shapes=[pltpu.VMEM((tm + 2 * HALO, CONV_WIDTH + POOL_WIDTH), F32),
                        pltpu.VMEM((tm, d), BF16)],
        compiler_params=pltpu.CompilerParams(
            dimension_semantics=("arbitrary", "arbitrary"), vmem_limit_bytes=VMEM_LIMIT),
        name="mix_in",
    )(x, x, x, g, w_in, w_conv, w_co, w_pool, pscale)


def _fft1_kernel(w_ref, wcc_ref, p_ref, o_ref):
    rows = FFT1_ROWS
    p = p_ref[0].astype(F32)
    a = []
    for s in range(FFT1_SUB):
        pf = p[:, s * SUBLANES:(s + 1) * SUBLANES, :].reshape(rows, FOURIER_WIDTH).astype(BF16)
        a.append(_dot(w_ref[...], pf).astype(BF16))
    tile = (DFT_N1, SUBLANES, FOURIER_GROUP)
    for q in range(N_GROUPS):
        cols = slice(q * FOURIER_GROUP, (q + 1) * FOURIER_GROUP)
        y = [_dot(jnp.concatenate([a_s[:rows, cols], a_s[rows:, cols]], axis=1), wcc_ref[...])
             for a_s in a]
        o_ref[0, 0, :, :, cols] = jnp.concatenate(
            [y_s[:, :FOURIER_GROUP].reshape(tile) for y_s in y], axis=1).astype(BF16)
        o_ref[0, 1, :, :, cols] = jnp.concatenate(
            [y_s[:, FOURIER_GROUP:].reshape(tile) for y_s in y], axis=1).astype(BF16)


def _fft1(p4, w1k, wcc):
    b = p4.shape[0]
    n2_blk = FFT1_SUB * SUBLANES
    return pl.pallas_call(
        _fft1_kernel,
        out_shape=jax.ShapeDtypeStruct((b, 2, DFT_N1, DFT_N2, FOURIER_WIDTH), BF16),
        grid=(b, DFT_N2 // n2_blk),
        in_specs=[
            _resident(w1k.shape),
            _resident(wcc.shape),
            pl.BlockSpec((1, DFT_N1, n2_blk, FOURIER_WIDTH), lambda bi, j: (bi, 0, j, 0)),
        ],
        out_specs=pl.BlockSpec((1, 2, DFT_N1, n2_blk, FOURIER_WIDTH),
                               lambda bi, j: (bi, 0, 0, j, 0)),
        compiler_params=pltpu.CompilerParams(
            dimension_semantics=("arbitrary", "arbitrary"), vmem_limit_bytes=VMEM_LIMIT),
        name="fft1",
    )(w1k, wcc, p4)


def _mix_out_kernel(x_ref, rest_ref, y_ref, gd_ref, g_ref, wga_ref, wgb_ref, wf_ref, wo_ref,
                    o_ref, m_ref, f_ref):
    t = OUT_TOK
    for j in range(SUBLANES):
        yj = jnp.concatenate([y_ref[0, 0, j], y_ref[0, 1, j]], axis=0)
        fj = _dot(gd_ref[0, j], yj)
        for c in range(FOURIER_WIDTH // LANES):
            f_ref[c, pl.ds(j, DFT_N2, stride=SUBLANES), :] = fj[:, c * LANES:(c + 1) * LANES]

    kh = DFT_N2 // 2
    th = t // 2
    for k0 in range(0, DFT_N2, kh):
        rows = slice(k0 * SUBLANES, k0 * SUBLANES + th)
        x = x_ref[0, k0:k0 + kh].reshape(th, D_MODEL)
        h = _rmsnorm_bf16(x, g_ref[...])
        rest = rest_ref[0, k0:k0 + kh].reshape(th, D_MODEL)
        for q in range(N_GROUPS):
            cols = slice(q * OUT_GROUP, (q + 1) * OUT_GROUP)
            wg_ref = wga_ref if q < GATE_BLK // OUT_GROUP else wgb_ref
            gc = (q * OUT_GROUP) % GATE_BLK
            gate_f = jax.nn.sigmoid(_dot(h, wg_ref[:, gc:gc + OUT_GROUP]))
            y_f = _dot(f_ref[q, rows, :].astype(BF16), wf_ref[q])
            m_ref[rows, cols] = (gate_f * y_f + rest[:, cols]).astype(BF16)
        o = x + _dot(m_ref[rows, :], wo_ref[...])
        o_ref[0, k0:k0 + kh] = o.reshape(kh, SUBLANES, D_MODEL)


def _mix_out(x4, rest4, y5, l, gd, g, w_in, wl, wf, wo):
    b = x4.shape[0]
    d = D_MODEL
    tok_blk = (1, DFT_N2, SUBLANES, d)
    tok = lambda bi, ki: (bi, 0, ki, 0)
    gate_col = OFF_G // GATE_BLK
    return pl.pallas_call(
        _mix_out_kernel,
        out_shape=jax.ShapeDtypeStruct(x4.shape, F32),
        grid=(b, DFT_N1 // SUBLANES),
        in_specs=[
            pl.BlockSpec(tok_blk, tok),
            pl.BlockSpec(tok_blk, tok),
            pl.BlockSpec((1, 2, SUBLANES, DFT_N2, FOURIER_WIDTH), lambda bi, ki: (bi, 0, ki, 0, 0)),
            pl.BlockSpec((1, SUBLANES, DFT_N2, 2 * DFT_N2), lambda bi, ki: (ki, 0, 0, 0)),
            _layer(l, (1, d)),
            _layer(wl, (d, GATE_BLK), gate_col),
            _layer(wl, (d, GATE_BLK), gate_col + 1),
            pl.BlockSpec((None, N_GROUPS, FOURIER_GROUP, OUT_GROUP), lambda *_: (wl, 0, 0, 0),
                         pipeline_mode=pl.Buffered(1)),
            _layer(wl, (d, d)),
        ],
        out_specs=pl.BlockSpec(tok_blk, tok),
        scratch_shapes=[pltpu.VMEM((OUT_TOK, d), BF16),
                        pltpu.VMEM((FOURIER_WIDTH // LANES, OUT_TOK, LANES), F32)],
        compiler_params=pltpu.CompilerParams(
            dimension_semantics=("arbitrary", "arbitrary"), vmem_limit_bytes=VMEM_LIMIT),
        name="mix_out",
    )(x4, rest4, y5, gd, g, w_in, w_in, wf, wo)


def _dft_tables():
    two_pi = 2.0 * np.pi
    r8 = SUBLANES
    c = np.arange(FOURIER_GROUP)
    ang = two_pi * ((c[:, None] * c[None, :]) % FOURIER_GROUP) / FOURIER_GROUP
    cs, sn = np.cos(ang), np.sin(ang)
    wcc = np.block([[cs, -sn], [sn, cs]])

    k1 = np.arange(DFT_N1)
    ang = two_pi * ((k1[:, None] * k1[None, :]) % DFT_N1) / DFT_N1
    w1 = np.stack([np.cos(ang), -np.sin(ang)], axis=0)
    eye = np.eye(r8)
    w1k = (w1[:, :, None, :, None] * eye[None, None, :, None, :]).reshape(2 * FFT1_ROWS, FFT1_ROWS)

    nkb = DFT_N1 // r8
    kk = (r8 * np.arange(nkb)[:, None, None] + np.arange(r8)[None, :, None]
          + DFT_N1 * np.arange(DFT_N2)[None, None, :])
    ang = two_pi * ((kk[..., None] * np.arange(DFT_N2)) % SEQ) / SEQ
    norm = 1.0 / math.sqrt(FOURIER_GROUP * SEQ)
    gd = (np.stack([np.cos(ang), np.sin(ang)], axis=3) * norm).reshape(nkb, r8, DFT_N2, 2 * DFT_N2)
    as_bf16 = lambda a: jnp.asarray(a.astype(np.float32)).astype(BF16)
    return as_bf16(wcc), as_bf16(w1k), as_bf16(gd)


def kernel(x, g_ffn1, w1_a, w3_a, w2_a, g_mix, w_in, w_fourier, w_conv, w_conv_out,
           w_pool, pool_scale, w_o, g_ffn2, w1_b, w3_b, w2_b, g_final):
    b, s, d = x.shape
    assert (s, d) == (SEQ, D_MODEL)
    wcc, w1k, gd = _dft_tables()
    row = lambda v: v.reshape(DEPTH, 1, d)
    gfin = g_final.reshape(1, d)
    g_ffn1, g_mix, g_ffn2, pool_scale = row(g_ffn1), row(g_mix), row(g_ffn2), row(pool_scale)
    grouped = (DEPTH, N_GROUPS * FOURIER_GROUP, OUT_GROUP)
    w_fourier, w_pool = w_fourier.reshape(grouped), w_pool.reshape(grouped)
    maps = (1, N_GROUPS, FOURIER_GROUP, OUT_GROUP)
    view = (b, DFT_N2, DFT_N1, d)
    ffn_a = (w1_a[:1].astype(BF16), w3_a[:1].astype(BF16), w2_a[:1].astype(BF16), 0)
    x = x.reshape(b * s, d)
    for l in range(DEPTH):
        x, (w_in_l, w1, w3, w2, w_o_l, w_co_l, w_f_l, w_p_l) = _ffn(
            x, l, g_ffn1, ffn_a, gfin, False,
            cast=((w_in, l), (w1_b, l), (w3_b, l), (w2_b, l),
                  (w_o, l), (w_conv_out, l), (w_fourier, l), (w_pool, l)))
        ffn_b = (w1, w3, w2, 0)
        x = x.reshape(b, s, d)
        pf, rest = _mix_in(x, l, g_mix, w_in_l, 0, w_conv, w_co_l, w_p_l.reshape(maps), pool_scale)
        y = _fft1(pf.reshape(b, DFT_N1, DFT_N2, FOURIER_WIDTH), w1k, wcc)
        x = _mix_out(x.reshape(view), rest.reshape(view), y, l, gd, g_mix, w_in_l, 0,
                     w_f_l.reshape(maps), w_o_l).reshape(b * s, d)
        nxt = l + 1
        cast = ((w1_a, nxt), (w3_a, nxt), (w2_a, nxt)) if nxt < DEPTH else ()
        x, cast_out = _ffn(x, l, g_ffn2, ffn_b, gfin, nxt == DEPTH, cast=cast)
        if cast_out:
            ffn_a = tuple(cast_out) + (0,)
    return x.reshape(b, s, d)
```
